```python
import math
import jax, jax.numpy as jnp
from jax import lax
import numpy as np

D_MODEL = 1024
BATCH = 8
SEQ = 2048
DEPTH = 2
DEC_BATCH = 128
DEC_SEQ = 8
PAST_LEN = 16384
PAGE_SIZE = 128

N_MIXERS = 2
N_MLSTM = (DEPTH + 1) // 2
N_CONV = DEPTH // 2
MLSTM_HEADS = 4
MLSTM_DK = D_MODEL // MLSTM_HEADS
MLSTM_DV = D_MODEL // MLSTM_HEADS
MLSTM_QK = MLSTM_HEADS * MLSTM_DK
MLSTM_V = MLSTM_HEADS * MLSTM_DV
MLSTM_CHUNK = 128
CONV_W = 3
D_FF = -(-8 * D_MODEL // (3 * 256)) * 256
EPS = 1e-6

kernel_name = "mlstm_shortconv_hybrid_step"


def rmsnorm(x, g):
    x32 = x.astype(jnp.float32)
    y = x32 * lax.rsqrt(jnp.mean(x32 * x32, axis=-1, keepdims=True) + EPS) * g.astype(jnp.float32)
    return y.astype(x.dtype)


def swiglu(xn, w_gu, w_down):
    gu = jnp.einsum('bsd,df->bsf', xn, w_gu)
    g, u = jnp.split(gu, 2, axis=-1)
    return jnp.einsum('bsf,fd->bsd', jax.nn.silu(g) * u, w_down)


def mlstm_scan(q, k, v, li, lf, C0, n0, m0):
    B, S, H, DK = q.shape
    DV = v.shape[-1]
    L = math.gcd(S, MLSTM_CHUNK)
    NC = S // L

    def blocks(a):
        a = a.reshape((B, NC, L) + a.shape[2:])
        return jnp.swapaxes(jnp.moveaxis(a, 1, 0), 2, 3)

    causal = jnp.tril(jnp.ones((L, L), dtype=bool))

    def step(carry, xs):
        C, n, m = carry
        qc, kc, vc, lic, lfc = xs
        b = jnp.cumsum(lfc, axis=-1)
        Dm = jnp.where(causal, b[..., :, None] - b[..., None, :] + lic[..., None, :], -jnp.inf)
        inter = b + m[..., None]
        m_eff = jnp.maximum(inter, jnp.max(Dm, axis=-1))
        w_inter = jnp.exp(inter - m_eff)
        s = jnp.einsum('bhld,bhsd->bhls', qc, kc) * jnp.exp(Dm - m_eff[..., None])
        num = w_inter[..., None] * jnp.einsum('bhvd,bhld->bhlv', C, qc) + jnp.einsum('bhls,bhsv->bhlv', s, vc)
        den = w_inter * jnp.einsum('bhd,bhld->bhl', n, qc) + jnp.sum(s, axis=-1)
        h = num / jnp.maximum(jnp.abs(den), jnp.exp(-m_eff))[..., None]
        bL = b[..., -1]
        g = bL[..., None] - b + lic
        m_new = jnp.maximum(bL + m, jnp.max(g, axis=-1))
        a = jnp.exp(bL + m - m_new)
        wg = jnp.exp(g - m_new[..., None])
        C_new = a[..., None, None] * C + jnp.einsum('bhl,bhlv,bhld->bhvd', wg, vc, kc)
        n_new = a[..., None] * n + jnp.einsum('bhl,bhld->bhd', wg, kc)
        return (C_new, n_new, m_new), h

    (C, n, m), h = lax.scan(step, (C0, n0, m0), (blocks(q), blocks(k), blocks(v), blocks(li), blocks(lf)))
    h = jnp.moveaxis(jnp.swapaxes(h, 2, 3), 0, 1).reshape(B, S, H, DV)
    return h, C, n, m


def mlstm_mixer(xn, w_in, b_gate, g_head, w_out, C0, n0, m0):
    B, S, _ = xn.shape
    f32 = jnp.float32
    proj = jnp.einsum('bsd,de->bse', xn, w_in).astype(f32)
    q = proj[..., :MLSTM_QK].reshape(B, S, MLSTM_HEADS, MLSTM_DK) * (MLSTM_DK ** -0.5)
    k = proj[..., MLSTM_QK:2 * MLSTM_QK].reshape(B, S, MLSTM_HEADS, MLSTM_DK)
    v = proj[..., 2 * MLSTM_QK:2 * MLSTM_QK + MLSTM_V].reshape(B, S, MLSTM_HEADS, MLSTM_DV)
    o = proj[..., 2 * MLSTM_QK + MLSTM_V:2 * MLSTM_QK + 2 * MLSTM_V]
    gates = proj[..., 2 * MLSTM_QK + 2 * MLSTM_V:] + b_gate.astype(f32)
    li = gates[..., :MLSTM_HEADS]
    lf = jax.nn.log_sigmoid(gates[..., MLSTM_HEADS:])
    h, C, n, m = mlstm_scan(q, k, v, li, lf, C0.astype(f32), n0.astype(f32), m0.astype(f32))
    h = h * lax.rsqrt(jnp.mean(h * h, axis=-1, keepdims=True) + EPS)
    h = h.reshape(B, S, MLSTM_V) * g_head.astype(f32) * jax.nn.sigmoid(o)
    y = jnp.einsum('bse,ed->bsd', h.astype(xn.dtype), w_out)
    return y, C, n, m


def conv_mixer(xn, w_in, w_taps, w_out, buf):
    S = xn.shape[1]
    proj = jnp.einsum('bsd,de->bse', xn, w_in)
    bg, cg, u = jnp.split(proj, 3, axis=-1)
    u = cg * u
    u_pad = jnp.concatenate([buf.astype(u.dtype), u], axis=1)
    conv = w_taps[0] * u_pad[:, 0:S]
    for j in range(1, CONV_W):
        conv = conv + w_taps[j] * u_pad[:, j:j + S]
    y = jnp.einsum('bsd,de->bse', bg * conv, w_out)
    return y, u_pad[:, -(CONV_W - 1):]


def trunk(x, st_C, st_n, st_m, st_conv, norm_mix, norm_ffn, norm_final, w_mlstm_in, b_mlstm_gate,
          mlstm_head_norm, w_mlstm_out, w_conv_in, w_conv_taps, w_conv_out, w_ffn_gate_up, w_ffn_down):
    Cs, ns, ms, bufs = [], [], [], []
    for layer in range(DEPTH):
        j = layer // N_MIXERS
        xn = rmsnorm(x, norm_mix[layer])
        if layer % N_MIXERS == 0:
            y, C, n, m = mlstm_mixer(xn, w_mlstm_in[j], b_mlstm_gate[j], mlstm_head_norm[j], w_mlstm_out[j],
                                     st_C[j], st_n[j], st_m[j])
            Cs.append(C); ns.append(n); ms.append(m)
        else:
            y, b = conv_mixer(xn, w_conv_in[j], w_conv_taps[j], w_conv_out[j], st_conv[j])
            bufs.append(b)
        x = x + y
        x = x + swiglu(rmsnorm(x, norm_ffn[layer]), w_ffn_gate_up[layer], w_ffn_down[layer])
    return rmsnorm(x, norm_final), jnp.stack(Cs), jnp.stack(ns), jnp.stack(ms), jnp.stack(bufs)


def setup_inputs(seed: int = 0) -> dict:
    key = jax.random.key(seed)
    ks = jax.random.split(key, 20)
    nrm = jax.random.normal
    H = MLSTM_HEADS
    d_in = 2 * MLSTM_QK + 2 * MLSTM_V + 2 * H
    b_gate = jnp.concatenate([
        -1.0 + 0.1 * nrm(ks[10], (N_MLSTM, H)),
        3.0 + 3.0 * jax.random.uniform(ks[11], (N_MLSTM, H))], axis=-1)
    return {
        "x_prompt": nrm(ks[0], (BATCH, SEQ, D_MODEL)),
        "x_sample": nrm(ks[1], (DEC_BATCH, DEC_SEQ, D_MODEL)),
        "state_mlstm_C": 0.1 * nrm(ks[2], (N_MLSTM, DEC_BATCH, H, MLSTM_DV, MLSTM_DK)),
        "state_mlstm_n": nrm(ks[3], (N_MLSTM, DEC_BATCH, H, MLSTM_DK)),
        "state_mlstm_m": nrm(ks[4], (N_MLSTM, DEC_BATCH, H)),
        "state_conv": nrm(ks[5], (N_CONV, DEC_BATCH, CONV_W - 1, D_MODEL)),
        "norm_mix": 1.0 + 0.05 * nrm(ks[6], (DEPTH, D_MODEL)),
        "norm_ffn": 1.0 + 0.05 * nrm(ks[7], (DEPTH, D_MODEL)),
        "norm_final": 1.0 + 0.05 * nrm(ks[8], (D_MODEL,)),
        "w_mlstm_in": nrm(ks[9], (N_MLSTM, D_MODEL, d_in)) * D_MODEL ** -0.5,
        "b_mlstm_gate": b_gate,
        "mlstm_head_norm": 1.0 + 0.05 * nrm(ks[12], (N_MLSTM, MLSTM_V)),
        "w_mlstm_out": nrm(ks[13], (N_MLSTM, MLSTM_V, D_MODEL)) * MLSTM_V ** -0.5,
        "w_conv_in": nrm(ks[14], (N_CONV, D_MODEL, 3 * D_MODEL)) * D_MODEL ** -0.5,
        "w_conv_taps": nrm(ks[15], (N_CONV, CONV_W, D_MODEL)) * CONV_W ** -0.5,
        "w_conv_out": nrm(ks[16], (N_CONV, D_MODEL, D_MODEL)) * D_MODEL ** -0.5,
        "w_ffn_gate_up": nrm(ks[17], (DEPTH, D_MODEL, 2 * D_FF)) * D_MODEL ** -0.5,
        "w_ffn_down": nrm(ks[18], (DEPTH, D_FF, D_MODEL)) * D_FF ** -0.5,
    }


def reference(x_prompt, x_sample, state_mlstm_C, state_mlstm_n, state_mlstm_m, state_conv,
              norm_mix, norm_ffn, norm_final, w_mlstm_in, b_mlstm_gate, mlstm_head_norm, w_mlstm_out,
              w_conv_in, w_conv_taps, w_conv_out, w_ffn_gate_up, w_ffn_down):
    Bp = x_prompt.shape[0]
    f32 = jnp.float32
    zC = jnp.zeros((N_MLSTM, Bp, MLSTM_HEADS, MLSTM_DV, MLSTM_DK), f32)
    zn = jnp.zeros((N_MLSTM, Bp, MLSTM_HEADS, MLSTM_DK), f32)
    zm = jnp.zeros((N_MLSTM, Bp, MLSTM_HEADS), f32)
    zconv = jnp.zeros((N_CONV, Bp, CONV_W - 1, D_MODEL), x_prompt.dtype)
    y_prompt, C_p, n_p, m_p, conv_p = trunk(
        x_prompt, zC, zn, zm, zconv, norm_mix, norm_ffn, norm_final, w_mlstm_in, b_mlstm_gate,
        mlstm_head_norm, w_mlstm_out, w_conv_in, w_conv_taps, w_conv_out, w_ffn_gate_up, w_ffn_down)
    y_sample, C_s, n_s, m_s, conv_s = trunk(
        x_sample, state_mlstm_C, state_mlstm_n, state_mlstm_m, state_conv, norm_mix, norm_ffn, norm_final,
        w_mlstm_in, b_mlstm_gate, mlstm_head_norm, w_mlstm_out, w_conv_in, w_conv_taps, w_conv_out,
        w_ffn_gate_up, w_ffn_down)
    return (y_prompt, y_sample, C_p, n_p, m_p, conv_p, C_s, n_s, m_s, conv_s)
```

```python
import functools

import jax
import jax.numpy as jnp
from jax import lax
from jax.experimental import pallas as pl
from jax.experimental.pallas import tpu as pltpu

EPS = 1e-6
N_HEADS = 4
MLSTM_CHUNK = 128
CONV_W = 3
LANES = 128
SUBLANES = 8
FFN_COLS = 256
VMEM_LIMIT = 56 * 1024 * 1024

F32 = jnp.float32
BF16 = jnp.bfloat16
_NT = (((1,), (1,)), ((), ()))
_TN = (((0,), (0,)), ((), ()))


def _params(n_grid):
    return pltpu.CompilerParams(
        dimension_semantics=("arbitrary",) * n_grid, vmem_limit_bytes=VMEM_LIMIT)


def _resident(shape):
    zeros = (0,) * len(shape)
    return pl.BlockSpec(shape, lambda *_: zeros, pipeline_mode=pl.Buffered(1))


def _rmsnorm(x, g):
    return x * lax.rsqrt(jnp.mean(x * x, axis=-1, keepdims=True) + EPS) * g


def _sigmoid(x):
    return 1.0 / (1.0 + jnp.exp(-x))


def _log_sigmoid(x):
    return jnp.minimum(x, 0.0) - jnp.log1p(jnp.exp(-jnp.abs(x)))


def _cumsum(x, axis):
    n = x.shape[axis]
    idx = lax.broadcasted_iota(jnp.int32, x.shape, axis)
    shift = 1
    while shift < n:
        x = x + jnp.where(idx >= shift, pltpu.roll(x, shift, axis), 0.0)
        shift *= 2
    return x


def _inproj_kernel(x_ref, g_ref, w_ref, wg_ref, bg_ref, proj_ref, gate_ref):
    d = x_ref.shape[1]
    xn = _rmsnorm(x_ref[...], g_ref[...]).astype(BF16)
    for c in range(w_ref.shape[1] // d):
        cols = slice(c * d, (c + 1) * d)
        proj_ref[:, cols] = jnp.dot(xn, w_ref[:, cols], preferred_element_type=F32).astype(BF16)
    gate_ref[...] = jnp.dot(xn, wg_ref[...], preferred_element_type=F32) + bg_ref[...]


def _inproj(x, g, w, wg, bg, tm):
    t, d = x.shape
    n = w.shape[1]
    return pl.pallas_call(
        _inproj_kernel,
        grid=(t // tm,),
        in_specs=[pl.BlockSpec((tm, d), lambda i: (i, 0)),
                  _resident((1, d)), _resident((d, n)), _resident((d, LANES)), _resident((1, LANES))],
        out_specs=[pl.BlockSpec((tm, n), lambda i: (i, 0)),
                   pl.BlockSpec((tm, LANES), lambda i: (i, 0))],
        out_shape=[jax.ShapeDtypeStruct((t, n), BF16), jax.ShapeDtypeStruct((t, LANES), F32)],
        compiler_params=_params(1),
        name="mlstm_inproj",
    )(x, g, w, wg, bg)


def _mlstm_head(q, k, v, li_c, li_r, b_c, b_r, m_prev, c_mat, n_row, causal):
    seq = q.shape[0]
    qf, kf, vf = q.astype(F32), k.astype(F32), v.astype(F32)
    b_last = b_c[seq - 1:seq, :]
    dm = jnp.where(causal, b_c - b_r + li_r, -jnp.inf)
    inter = b_c + m_prev
    m_eff = jnp.maximum(inter, jnp.max(dm, axis=1, keepdims=True))
    w_inter = jnp.exp(inter - m_eff)
    s = lax.dot_general(q, k, _NT, preferred_element_type=F32) * jnp.exp(dm - m_eff)
    qc = lax.dot_general(q, c_mat.astype(q.dtype), _NT, preferred_element_type=F32)
    num = w_inter * qc + jnp.dot(s.astype(v.dtype), v, preferred_element_type=F32)
    den = w_inter * jnp.sum(qf * n_row, axis=1, keepdims=True) + jnp.sum(s, axis=1, keepdims=True)
    h = num * (1.0 / jnp.maximum(jnp.abs(den), jnp.exp(-m_eff)))
    g = b_last - b_c + li_c
    m_new = jnp.maximum(b_last + m_prev, jnp.max(g, axis=0, keepdims=True))
    decay = jnp.exp(b_last + m_prev - m_new)
    wg = jnp.exp(g - m_new)
    upd = lax.dot_general((vf * wg).astype(v.dtype), k, _TN, preferred_element_type=F32)
    c_new = decay * c_mat + upd
    n_new = decay * n_row + jnp.sum(kf * wg, axis=0, keepdims=True)
    return h, c_new, n_new, m_new


def _gated_head_out(h, g_head, o):
    hn = h * lax.rsqrt(jnp.mean(h * h, axis=-1, keepdims=True) + EPS)
    return hn * g_head * _sigmoid(o)


def _scan_prompt_kernel(q_ref, k_ref, v_ref, o_ref, gate_ref, gh_ref,
                        hg_ref, c_ref, n_ref, m_ref):
    seq = gate_ref.shape[0]
    dk = q_ref.shape[1] // N_HEADS

    @pl.when(pl.program_id(1) == 0)
    def _():
        c_ref[...] = jnp.zeros_like(c_ref)
        n_ref[...] = jnp.zeros_like(n_ref)
        m_ref[...] = jnp.zeros_like(m_ref)

    gates = gate_ref[...]
    gates_t = gates.T
    b_col = _cumsum(_log_sigmoid(gates), 0)
    b_row = _cumsum(_log_sigmoid(gates_t), 1)
    causal = (lax.broadcasted_iota(jnp.int32, (seq, seq), 0)
              >= lax.broadcasted_iota(jnp.int32, (seq, seq), 1))
    for h in range(N_HEADS):
        cols = slice(h * dk, (h + 1) * dk)
        f = N_HEADS + h
        hh, c_new, n_new, m_new = _mlstm_head(
            q_ref[:, cols], k_ref[:, cols], v_ref[:, cols],
            gates[:, h:h + 1], gates_t[h:h + 1, :], b_col[:, f:f + 1], b_row[f:f + 1, :],
            m_ref[0, h:h + 1, 0:1], c_ref[0, h], n_ref[0, h:h + 1, :], causal)
        c_ref[0, h] = c_new
        n_ref[0, h:h + 1, :] = n_new
        m_ref[0, h:h + 1, :] = jnp.broadcast_to(m_new, (1, LANES))
        hg_ref[:, cols] = _gated_head_out(hh, gh_ref[:, cols], o_ref[:, cols].astype(F32)).astype(BF16)


def _scan_prompt(proj, gates, g_head, batch, seq):
    d = g_head.shape[1]
    dk = d // N_HEADS
    chunk = MLSTM_CHUNK
    nc = seq // chunk
    tok = lambda j: pl.BlockSpec((chunk, d), lambda b, c: (b * nc + c, j))
    return pl.pallas_call(
        _scan_prompt_kernel,
        grid=(batch, nc),
        in_specs=[tok(0), tok(1), tok(2), tok(3),
                  pl.BlockSpec((chunk, LANES), lambda b, c: (b * nc + c, 0)),
                  _resident((1, d))],
        out_specs=[tok(0),
                   pl.BlockSpec((1, N_HEADS, dk, dk), lambda b, c: (b, 0, 0, 0)),
                   pl.BlockSpec((1, SUBLANES, dk), lambda b, c: (b, 0, 0)),
                   pl.BlockSpec((1, SUBLANES, LANES), lambda b, c: (b, 0, 0))],
        out_shape=[jax.ShapeDtypeStruct((batch * seq, d), BF16),
                   jax.ShapeDtypeStruct((batch, N_HEADS, dk, dk), F32),
                   jax.ShapeDtypeStruct((batch, SUBLANES, dk), F32),
                   jax.ShapeDtypeStruct((batch, SUBLANES, LANES), F32)],
        compiler_params=_params(2),
        name="mlstm_scan_prompt",
    )(proj, proj, proj, proj, gates, g_head)


def _scan_sample_kernel(q_ref, k_ref, v_ref, o_ref, gate_ref, gh_ref, c_in, n_in, m_in,
                        hg_ref, c_out, n_out, m_out, *, seq):
    bt = c_in.shape[0]
    dk = q_ref.shape[1] // N_HEADS
    step = pl.program_id(0)
    row = lax.broadcasted_iota(jnp.int32, (seq, seq), 0)
    col = lax.broadcasted_iota(jnp.int32, (seq, seq), 1)
    causal = row >= col
    diag = row == col
    lane = lax.broadcasted_iota(jnp.int32, (1, LANES), 1)
    to_row = lambda x_col: jnp.sum(jnp.where(diag, x_col, 0.0), axis=0, keepdims=True)

    qs, ks, vs = q_ref[...].astype(F32), k_ref[...].astype(F32), v_ref[...].astype(F32)
    os_ = o_ref[...].astype(F32)
    for j in range(bt):
        rows = slice(j * seq, (j + 1) * seq)
        gates = gate_ref[rows, :]
        b_col = _cumsum(_log_sigmoid(gates), 0)
        m_prev_row = m_in[pl.ds(step * bt + j, 1), :]
        m_new_row = jnp.zeros((1, LANES), F32)
        for h in range(N_HEADS):
            cols = slice(h * dk, (h + 1) * dk)
            f = N_HEADS + h
            li_c, b_c = gates[:, h:h + 1], b_col[:, f:f + 1]
            hh, c_new, n_new, m_new = _mlstm_head(
                qs[rows, cols], ks[rows, cols], vs[rows, cols],
                li_c, to_row(li_c), b_c, to_row(b_c),
                m_prev_row[:, h:h + 1], c_in[j, h], n_in[j, h:h + 1, :], causal)
            c_out[j, h] = c_new
            n_out[j, h:h + 1, :] = n_new
            m_new_row = jnp.where(lane == h, m_new, m_new_row)
            hg_ref[rows, cols] = _gated_head_out(hh, gh_ref[:, cols], os_[rows, cols])
        m_out[pl.ds(step * bt + j, 1), :] = m_new_row


def _scan_sample(proj, gates, g_head, c0, n0, m0, seq, bt):
    batch = c0.shape[0]
    d = g_head.shape[1]
    dk = d // N_HEADS
    tok = lambda j: pl.BlockSpec((bt * seq, d), lambda i: (i, j))
    c_spec = pl.BlockSpec((bt, N_HEADS, dk, dk), lambda i: (i, 0, 0, 0))
    n_spec = pl.BlockSpec((bt, N_HEADS, dk), lambda i: (i, 0, 0))
    m_spec = pl.BlockSpec((batch, LANES), lambda i: (0, 0))
    return pl.pallas_call(
        functools.partial(_scan_sample_kernel, seq=seq),
        grid=(batch // bt,),
        in_specs=[tok(0), tok(1), tok(2), tok(3),
                  pl.BlockSpec((bt * seq, LANES), lambda i: (i, 0)),
                  _resident((1, d)), c_spec, n_spec, m_spec],
        out_specs=[tok(0), c_spec, n_spec, m_spec],
        out_shape=[jax.ShapeDtypeStruct((batch * seq, d), F32),
                   jax.ShapeDtypeStruct(c0.shape, F32),
                   jax.ShapeDtypeStruct(n0.shape, F32),
                   jax.ShapeDtypeStruct((batch, LANES), F32)],
        compiler_params=_params(1),
        name="mlstm_scan_sample",
    )(proj, proj, proj, proj, gates, g_head, c0, n0, m0)


def _post_kernel(x_ref, z_ref, wo_ref, g_ref, wgu_ref, wd_ref, gfin_ref, out_ref,
                 xn_ref, act_ref, *, final_norm):
    x1 = x_ref[...] + jnp.dot(z_ref[...].astype(BF16), wo_ref[...], preferred_element_type=F32)
    out_ref[...] = x1
    xn_ref[...] = _rmsnorm(x1, g_ref[...]).astype(BF16)
    ff = wd_ref.shape[0]
    for c in range(ff // FFN_COLS):
        gate = jnp.dot(xn_ref[...], wgu_ref[:, c * FFN_COLS:(c + 1) * FFN_COLS],
                       preferred_element_type=F32)
        up = jnp.dot(xn_ref[...], wgu_ref[:, ff + c * FFN_COLS:ff + (c + 1) * FFN_COLS],
                     preferred_element_type=F32)
        act_ref[:, c * FFN_COLS:(c + 1) * FFN_COLS] = (gate * _sigmoid(gate) * up).astype(BF16)
    x2 = out_ref[...] + jnp.dot(act_ref[...], wd_ref[...], preferred_element_type=F32)
    if final_norm:
        x2 = _rmsnorm(x2, gfin_ref[...])
    out_ref[...] = x2


def _post(x, z, wo, g, wgu, wd, gfin, tm, final_norm):
    t, d = x.shape
    ff = wd.shape[0]
    assert ff % FFN_COLS == 0
    tile = pl.BlockSpec((tm, d), lambda i: (i, 0))
    return pl.pallas_call(
        functools.partial(_post_kernel, final_norm=final_norm),
        grid=(t // tm,),
        in_specs=[tile, tile, _resident((d, d)), _resident((1, d)),
                  _resident((d, 2 * ff)), _resident((ff, d)), _resident((1, d))],
        out_specs=tile,
        out_shape=jax.ShapeDtypeStruct((t, d), F32),
        scratch_shapes=[pltpu.VMEM((tm, d), BF16), pltpu.VMEM((tm, ff), BF16)],
        compiler_params=_params(1),
        name="outproj_ffn_final" if final_norm else "outproj_ffn",
    )(x, z, wo, g, wgu, wd, gfin)


def _conv_taps(u, prev1, prev2, taps_ref):
    return taps_ref[0:1, :] * prev2 + taps_ref[1:2, :] * prev1 + taps_ref[2:3, :] * u


def _conv_proj(x_ref, g_ref, w_ref):
    d = x_ref.shape[1]
    xn = _rmsnorm(x_ref[...], g_ref[...]).astype(BF16)
    part = lambda c: jnp.dot(xn, w_ref[:, c * d:(c + 1) * d], preferred_element_type=F32)
    return part(0), part(1) * part(2)


def _conv_long_kernel(x_ref, g_ref, w_ref, taps_ref, st_ref, z_ref, st_out_ref, carry_ref):
    tm = x_ref.shape[0]

    @pl.when(pl.program_id(1) == 0)
    def _():
        carry_ref[...] = st_ref[0]

    bg, u = _conv_proj(x_ref, g_ref, w_ref)
    row = lax.broadcasted_iota(jnp.int32, (tm, 1), 0)
    c0, c1 = carry_ref[0:1, :], carry_ref[1:2, :]
    prev1 = jnp.where(row == 0, c1, pltpu.roll(u, 1, 0))
    prev2 = jnp.where(row == 0, c0, jnp.where(row == 1, c1, pltpu.roll(u, 2, 0)))
    z_ref[...] = (bg * _conv_taps(u, prev1, prev2, taps_ref)).astype(BF16)
    tail = u[tm - (CONV_W - 1):tm, :]
    carry_ref[...] = tail
    st_out_ref[0] = tail


def _conv_short_kernel(x_ref, g_ref, w_ref, taps_ref, st_ref, z_ref, st_out_ref, *, seq):
    tm, d = x_ref.shape
    nseq = tm // seq
    bg, u = _conv_proj(x_ref, g_ref, w_ref)
    pos = lax.broadcasted_iota(jnp.int32, (1, seq, 1), 1)
    s0, s1 = st_ref[:, 0:1, :], st_ref[:, 1:2, :]
    u3 = u.reshape(nseq, seq, d)
    r1 = pltpu.roll(u, 1, 0).reshape(nseq, seq, d)
    r2 = pltpu.roll(u, 2, 0).reshape(nseq, seq, d)
    prev1 = jnp.where(pos == 0, s1, r1)
    prev2 = jnp.where(pos == 0, s0, jnp.where(pos == 1, s1, r2))
    conv = (taps_ref[0:1, :].reshape(1, 1, d) * prev2 + taps_ref[1:2, :].reshape(1, 1, d) * prev1
            + taps_ref[2:3, :].reshape(1, 1, d) * u3)
    z_ref[...] = (bg * conv.reshape(tm, d)).astype(BF16)
    st_out_ref[...] = u3[:, seq - (CONV_W - 1):seq, :]


def _conv(x, g, w, taps, state, batch, seq, tm):
    t, d = x.shape
    keep = CONV_W - 1
    out_shape = [jax.ShapeDtypeStruct((t, d), BF16), jax.ShapeDtypeStruct((batch, keep, d), F32)]
    weights = [_resident((1, d)), _resident((d, 3 * d)), _resident((SUBLANES, d))]
    if seq >= tm:
        nt = seq // tm
        st_spec = pl.BlockSpec((1, keep, d), lambda b, j: (b, 0, 0))
        tile = pl.BlockSpec((tm, d), lambda b, j: (b * nt + j, 0))
        return pl.pallas_call(
            _conv_long_kernel,
            grid=(batch, nt),
            in_specs=[tile] + weights + [st_spec],
            out_specs=[tile, st_spec],
            out_shape=out_shape,
            scratch_shapes=[pltpu.VMEM((keep, d), F32)],
            compiler_params=_params(2),
            name="gated_conv_long",
        )(x, g, w, taps, state)
    nseq = tm // seq
    st_spec = pl.BlockSpec((nseq, keep, d), lambda i: (i, 0, 0))
    tile = pl.BlockSpec((tm, d), lambda i: (i, 0))
    return pl.pallas_call(
        functools.partial(_conv_short_kernel, seq=seq),
        grid=(t // tm,),
        in_specs=[tile] + weights + [st_spec],
        out_specs=[tile, st_spec],
        out_shape=out_shape,
        compiler_params=_params(1),
        name="gated_conv_short",
    )(x, g, w, taps, state)


def _token_tile(t):
    return min(t, 512)


def _trunk(x, c0, n0, m0, conv0, wts, sample_bt):
    batch, seq, d = x.shape
    t = batch * seq
    tm = _token_tile(t)
    x2d = x.reshape(t, d)
    proj, gates = _inproj(x2d, wts["norm_mix0"], wts["w_in"], wts["w_gate"], wts["b_gate"], tm)
    if c0 is None:
        hg, c_fin, n_pad, m_pad = _scan_prompt(proj, gates, wts["g_head"], batch, seq)
        n_fin, m_fin = n_pad[:, :N_HEADS, :], m_pad[:, :N_HEADS, 0]
    else:
        m0_pad = jnp.pad(m0, ((0, 0), (0, LANES - N_HEADS)))
        hg, c_fin, n_fin, m_pad = _scan_sample(proj, gates, wts["g_head"], c0, n0, m0_pad, seq, sample_bt)
        m_fin = m_pad[:, :N_HEADS]
    x1 = _post(x2d, hg, wts["w_out0"], wts["norm_ffn0"], wts["w_gu0"], wts["w_down0"],
               wts["norm_final"], tm, final_norm=False)
    z, conv_fin = _conv(x1, wts["norm_mix1"], wts["w_conv_in"], wts["taps"], conv0, batch, seq, tm)
    y = _post(x1, z, wts["w_conv_out"], wts["norm_ffn1"], wts["w_gu1"], wts["w_down1"],
              wts["norm_final"], tm, final_norm=True)
    return (y.reshape(batch, seq, d), c_fin[None], n_fin[None], m_fin[None], conv_fin[None])


def kernel(x_prompt, x_sample, state_mlstm_C, state_mlstm_n, state_mlstm_m, state_conv, norm_mix, norm_ffn, norm_final, w_mlstm_in, b_mlstm_gate, mlstm_head_norm, w_mlstm_out, w_conv_in, w_conv_taps, w_conv_out, w_ffn_gate_up, w_ffn_down):
    d = x_prompt.shape[-1]
    dk = d // N_HEADS
    assert norm_mix.shape[0] == 2 and w_mlstm_in.shape[0] == 1 and w_conv_in.shape[0] == 1
    assert w_conv_taps.shape[1] == CONV_W and 2 * N_HEADS <= LANES
    row = lambda v: v.reshape(1, -1).astype(F32)
    w_in = w_mlstm_in[0]
    q_scale = jnp.concatenate([jnp.full((d,), dk ** -0.5, F32), jnp.ones((3 * d,), F32)])
    wts = {
        "norm_mix0": row(norm_mix[0]), "norm_mix1": row(norm_mix[1]),
        "norm_ffn0": row(norm_ffn[0]), "norm_ffn1": row(norm_ffn[1]),
        "norm_final": row(norm_final), "g_head": row(mlstm_head_norm[0]),
        "w_in": (w_in[:, :4 * d] * q_scale).astype(BF16),
        "w_gate": jnp.pad(w_in[:, 4 * d:], ((0, 0), (0, LANES - 2 * N_HEADS))).astype(BF16),
        "b_gate": jnp.pad(b_mlstm_gate[0], (0, LANES - 2 * N_HEADS)).reshape(1, LANES).astype(F32),
        "w_out0": w_mlstm_out[0].astype(BF16),
        "w_conv_in": w_conv_in[0].astype(BF16),
        "w_conv_out": w_conv_out[0].astype(BF16),
        "taps": jnp.pad(w_conv_taps[0], ((0, SUBLANES - CONV_W), (0, 0))).astype(F32),
        "w_gu0": w_ffn_gate_up[0].astype(BF16), "w_gu1": w_ffn_gate_up[1].astype(BF16),
        "w_down0": w_ffn_down[0].astype(BF16), "w_down1": w_ffn_down[1].astype(BF16),
    }
    bp = x_prompt.shape[0]
    zero_conv = jnp.zeros((bp, CONV_W - 1, d), F32)
    y_p, c_p, n_p, m_p, conv_p = _trunk(x_prompt, None, None, None, zero_conv, wts, None)
    y_s, c_s, n_s, m_s, conv_s = _trunk(
        x_sample, state_mlstm_C[0], state_mlstm_n[0], state_mlstm_m[0], state_conv[0], wts, 4)
    return (y_p, y_s, c_p, n_p, m_p, conv_p, c_s, n_s, m_s, conv_s)
```

```python
import functools

import jax
import jax.numpy as jnp
from jax import lax
from jax.experimental import pallas as pl
from jax.experimental.pallas import tpu as pltpu

EPS = 1e-6
N_HEADS = 4
MLSTM_CHUNK = 128
CONV_W = 3
LANES = 128
SUBLANES = 8
GATE_COLS = 2 * LANES
FFN_COLS = 256
VMEM_LIMIT = 56 * 1024 * 1024

F32 = jnp.float32
BF16 = jnp.bfloat16
_NT = (((1,), (1,)), ((), ()))
_TN = (((0,), (0,)), ((), ()))


def _params(n_grid):
    return pltpu.CompilerParams(
        dimension_semantics=("arbitrary",) * n_grid, vmem_limit_bytes=VMEM_LIMIT)


def _resident(shape):
    zeros = (0,) * len(shape)
    return pl.BlockSpec(shape, lambda *_: zeros, pipeline_mode=pl.Buffered(1))


def _rmsnorm(x, g):
    return x * lax.rsqrt(jnp.mean(x * x, axis=-1, keepdims=True) + EPS) * g


def _sigmoid(x):
    return 1.0 / (1.0 + jnp.exp(-x))


def _log_sigmoid(x):
    return jnp.minimum(x, 0.0) - jnp.log1p(jnp.exp(-jnp.abs(x)))


def _prefix(x, axis, combine, identity):
    n = x.shape[axis]
    idx = lax.broadcasted_iota(jnp.int32, x.shape, axis)
    shift = 1
    while shift < n:
        x = combine(x, jnp.where(idx >= shift, pltpu.roll(x, shift, axis), identity))
        shift *= 2
    return x


def _cumsum(x, axis):
    return _prefix(x, axis, jnp.add, 0.0)


def _inproj_kernel(x_ref, g_ref, w_ref, wg_ref, bg_ref, proj_ref, gate_ref):
    d = x_ref.shape[1]
    xn = _rmsnorm(x_ref[...], g_ref[...]).astype(BF16)
    n_blocks = w_ref.shape[1] // d
    for c in range(n_blocks):
        cols = slice(c * d, (c + 1) * d)
        y = jnp.dot(xn, w_ref[:, cols], preferred_element_type=F32)
        if c == n_blocks - 1:
            y = _sigmoid(y)
        proj_ref[:, cols] = y.astype(BF16)
    gate_ref[...] = jnp.dot(xn, wg_ref[...], preferred_element_type=F32) + bg_ref[...]


def _inproj(x, g, w, wg, bg, tm):
    t, d = x.shape
    n = w.shape[1]
    return pl.pallas_call(
        _inproj_kernel,
        grid=(t // tm,),
        in_specs=[pl.BlockSpec((tm, d), lambda i: (i, 0)),
                  _resident((1, d)), _resident((d, n)), _resident((d, GATE_COLS)), _resident((1, GATE_COLS))],
        out_specs=[pl.BlockSpec((tm, n), lambda i: (i, 0)),
                   pl.BlockSpec((tm, GATE_COLS), lambda i: (i, 0))],
        out_shape=[jax.ShapeDtypeStruct((t, n), BF16), jax.ShapeDtypeStruct((t, GATE_COLS), F32)],
        compiler_params=_params(1),
        name="mlstm_inproj",
    )(x, g, w, wg, bg)


def _mlstm_head(q, k, v, li_c, li_r, b_c, b_r, m_prev, c_mat, n_row, causal):
    seq = q.shape[0]
    qf, kf, vf = q.astype(F32), k.astype(F32), v.astype(F32)
    b_last = b_c[seq - 1:seq, :]
    dm = jnp.where(causal, b_c - b_r + li_r, -jnp.inf)
    inter = b_c + m_prev
    m_eff = jnp.maximum(inter, jnp.max(dm, axis=1, keepdims=True))
    w_inter = jnp.exp(inter - m_eff)
    s = lax.dot_general(q, k, _NT, preferred_element_type=F32) * jnp.exp(dm - m_eff)
    qc = lax.dot_general(q, c_mat.astype(q.dtype), _NT, preferred_element_type=F32)
    num = w_inter * qc + jnp.dot(s.astype(v.dtype), v, preferred_element_type=F32)
    den = w_inter * jnp.sum(qf * n_row, axis=1, keepdims=True) + jnp.sum(s, axis=1, keepdims=True)
    h = num * (1.0 / jnp.maximum(jnp.abs(den), jnp.exp(-m_eff)))
    g = b_last - b_c + li_c
    m_new = jnp.maximum(b_last + m_prev, jnp.max(g, axis=0, keepdims=True))
    decay = jnp.exp(b_last + m_prev - m_new)
    wg = jnp.exp(g - m_new)
    upd = lax.dot_general((vf * wg).astype(v.dtype), k, _TN, preferred_element_type=F32)
    c_new = decay * c_mat + upd
    n_new = decay * n_row + jnp.sum(kf * wg, axis=0, keepdims=True)
    return h, c_new, n_new, m_new


def _gated_head_out(h, g_head, sig_o):
    hn = h * lax.rsqrt(jnp.mean(h * h, axis=-1, keepdims=True) + EPS)
    return hn * g_head * sig_o


def _per_row(rep, width):
    return jnp.concatenate([rep] * (width // LANES), axis=1)


def _scan_prompt_kernel(q_ref, k_ref, v_ref, so_ref, gate_ref, gh_ref,
                        hg_ref, c_ref, n_ref, m_ref):
    seq = gate_ref.shape[0]
    dk = q_ref.shape[1] // N_HEADS
    assert seq == LANES

    @pl.when(pl.program_id(1) == 0)
    def _():
        c_ref[...] = jnp.zeros_like(c_ref)
        n_ref[...] = jnp.zeros_like(n_ref)
        m_ref[...] = jnp.zeros_like(m_ref)

    lane = lax.broadcasted_iota(jnp.int32, (1, LANES), 1)
    li = gate_ref[:, 0:LANES]
    b = _prefix(_log_sigmoid(gate_ref[:, LANES:GATE_COLS]), 0, jnp.add, 0.0)
    a = li - b
    m_prev = m_ref[0, 0:1, :]
    neg_mx = -jnp.maximum(m_prev, _prefix(a, 0, jnp.maximum, -jnp.inf))
    exp_neg_meff = jnp.exp(neg_mx - b)
    b_last = b[seq - 1:seq, :]
    g = b_last - b + li
    m_new = jnp.maximum(b_last + m_prev, jnp.max(g, axis=0, keepdims=True))
    decay_row = jnp.exp(b_last + m_prev - m_new)
    wg_t = jnp.exp(g - m_new).T
    a_t = a.T
    m_ref[0, 0:1, :] = jnp.where(lane < N_HEADS, m_new, 0.0)

    causal = (lax.broadcasted_iota(jnp.int32, (seq, seq), 0)
              >= lax.broadcasted_iota(jnp.int32, (seq, seq), 1))
    ones_l = jnp.ones((seq, LANES), BF16)
    ones_v = jnp.ones((dk, LANES), BF16)
    wg16 = wg_t[0:2 * SUBLANES, :].astype(BF16)
    nums = []
    den = jnp.zeros((seq, LANES), F32)
    for h in range(N_HEADS):
        cols = slice(h * dk, (h + 1) * dk)
        q, k, v = q_ref[:, cols], k_ref[:, cols], v_ref[:, cols]
        c_mat, n_row = c_ref[0, h], n_ref[0, h:h + 1, :]
        neg_mx_rep = jnp.broadcast_to(neg_mx[:, h:h + 1], (seq, LANES))
        p = jnp.exp(jnp.where(causal, neg_mx_rep + a_t[h:h + 1, :], -jnp.inf))
        w_rep = jnp.exp(neg_mx_rep + m_prev[:, h:h + 1])
        rhs = jnp.concatenate(
            [k, c_mat.astype(BF16), jnp.broadcast_to(n_row.astype(BF16), (LANES, dk))], axis=0)
        qx = lax.dot_general(q, rhs, _NT, preferred_element_type=F32)
        s = (qx[:, :seq] * p).astype(BF16)
        sv = jnp.dot(s, jnp.concatenate([v, ones_l], axis=1), preferred_element_type=F32)
        nums.append(_per_row(w_rep, dk) * qx[:, seq:seq + dk] + sv[:, :dk])
        den = jnp.where(lane == h, w_rep * qx[:, seq + dk:] + sv[:, dk:], den)
        vtw = (v.T.astype(F32) * wg_t[h:h + 1, :]).astype(BF16)
        upd = jnp.dot(jnp.concatenate([vtw, wg16], axis=0), k, preferred_element_type=F32)
        decay = decay_row[:, h:h + 1]
        c_ref[0, h] = decay * c_mat + upd[:dk]
        n_ref[0, h:h + 1, :] = decay * n_row + upd[dk + h:dk + h + 1, :]
    recip = 1.0 / jnp.maximum(jnp.abs(den), exp_neg_meff)
    for h in range(N_HEADS):
        cols = slice(h * dk, (h + 1) * dk)
        hh = nums[h] * _per_row(jnp.broadcast_to(recip[:, h:h + 1], (seq, LANES)), dk)
        ssq_rep = jnp.dot((hh * hh).astype(BF16), ones_v, preferred_element_type=F32)
        hn = hh * _per_row(lax.rsqrt(ssq_rep * (1.0 / dk) + EPS), dk)
        hg_ref[:, cols] = (hn * gh_ref[:, cols] * so_ref[:, cols].astype(F32)).astype(BF16)


def _scan_prompt(proj, gates, g_head, batch, seq):
    d = g_head.shape[1]
    dk = d // N_HEADS
    chunk = MLSTM_CHUNK
    nc = seq // chunk
    tok = lambda j: pl.BlockSpec((chunk, d), lambda b, c: (b * nc + c, j))
    return pl.pallas_call(
        _scan_prompt_kernel,
        grid=(batch, nc),
        in_specs=[tok(0), tok(1), tok(2), tok(3),
                  pl.BlockSpec((chunk, GATE_COLS), lambda b, c: (b * nc + c, 0)),
                  _resident((1, d))],
        out_specs=[tok(0),
                   pl.BlockSpec((1, N_HEADS, dk, dk), lambda b, c: (b, 0, 0, 0)),
                   pl.BlockSpec((1, SUBLANES, dk), lambda b, c: (b, 0, 0)),
                   pl.BlockSpec((1, SUBLANES, LANES), lambda b, c: (b, 0, 0))],
        out_shape=[jax.ShapeDtypeStruct((batch * seq, d), BF16),
                   jax.ShapeDtypeStruct((batch, N_HEADS, dk, dk), F32),
                   jax.ShapeDtypeStruct((batch, SUBLANES, dk), F32),
                   jax.ShapeDtypeStruct((batch, SUBLANES, LANES), F32)],
        compiler_params=_params(2),
        name="mlstm_scan_prompt",
    )(proj, proj, proj, proj, gates, g_head)


def _scan_sample_kernel(q_ref, k_ref, v_ref, so_ref, gate_ref, gh_ref, c_in, n_in, m_in,
                        hg_ref, c_out, n_out, m_out, *, seq):
    bt = c_in.shape[0]
    dk = q_ref.shape[1] // N_HEADS
    step = pl.program_id(0)
    row = lax.broadcasted_iota(jnp.int32, (seq, seq), 0)
    col = lax.broadcasted_iota(jnp.int32, (seq, seq), 1)
    causal = row >= col
    diag = row == col
    lane = lax.broadcasted_iota(jnp.int32, (1, LANES), 1)
    to_row = lambda x_col: jnp.sum(jnp.where(diag, x_col, 0.0), axis=0, keepdims=True)

    qs, ks, vs = q_ref[...].astype(F32), k_ref[...].astype(F32), v_ref[...].astype(F32)
    sig_o = so_ref[...].astype(F32)
    for j in range(bt):
        rows = slice(j * seq, (j + 1) * seq)
        gates = gate_ref[rows, :]
        b_col = _cumsum(_log_sigmoid(gates), 0)
        m_prev_row = m_in[pl.ds(step * bt + j, 1), :]
        m_new_row = jnp.zeros((1, LANES), F32)
        for h in range(N_HEADS):
            cols = slice(h * dk, (h + 1) * dk)
            f = LANES + h
            li_c, b_c = gates[:, h:h + 1], b_col[:, f:f + 1]
            hh, c_new, n_new, m_new = _mlstm_head(
                qs[rows, cols], ks[rows, cols], vs[rows, cols],
                li_c, to_row(li_c), b_c, to_row(b_c),
                m_prev_row[:, h:h + 1], c_in[j, h], n_in[j, h:h + 1, :], causal)
            c_out[j, h] = c_new
            n_out[j, h:h + 1, :] = n_new
            m_new_row = jnp.where(lane == h, m_new, m_new_row)
            hg_ref[rows, cols] = _gated_head_out(hh, gh_ref[:, cols], sig_o[rows, cols])
        m_out[pl.ds(step * bt + j, 1), :] = m_new_row


def _scan_sample(proj, gates, g_head, c0, n0, m0, seq, bt):
    batch = c0.shape[0]
    d = g_head.shape[1]
    dk = d // N_HEADS
    tok = lambda j: pl.BlockSpec((bt * seq, d), lambda i: (i, j))
    c_spec = pl.BlockSpec((bt, N_HEADS, dk, dk), lambda i: (i, 0, 0, 0))
    n_spec = pl.BlockSpec((bt, N_HEADS, dk), lambda i: (i, 0, 0))
    m_spec = pl.BlockSpec((batch, LANES), lambda i: (0, 0))
    return pl.pallas_call(
        functools.partial(_scan_sample_kernel, seq=seq),
        grid=(batch // bt,),
        in_specs=[tok(0), tok(1), tok(2), tok(3),
                  pl.BlockSpec((bt * seq, GATE_COLS), lambda i: (i, 0)),
                  _resident((1, d)), c_spec, n_spec, m_spec],
        out_specs=[tok(0), c_spec, n_spec, m_spec],
        out_shape=[jax.ShapeDtypeStruct((batch * seq, d), F32),
                   jax.ShapeDtypeStruct(c0.shape, F32),
                   jax.ShapeDtypeStruct(n0.shape, F32),
                   jax.ShapeDtypeStruct((batch, LANES), F32)],
        compiler_params=_params(1),
        name="mlstm_scan_sample",
    )(proj, proj, proj, proj, gates, g_head, c0, n0, m0)


def _post_kernel(x_ref, z_ref, wo_ref, g_ref, wgu_ref, wd_ref, gfin_ref, out_ref,
                 xn_ref, act_ref, *, final_norm):
    x1 = x_ref[...] + jnp.dot(z_ref[...].astype(BF16), wo_ref[...], preferred_element_type=F32)
    out_ref[...] = x1
    xn_ref[...] = _rmsnorm(x1, g_ref[...]).astype(BF16)
    ff = wd_ref.shape[0]
    for c in range(ff // FFN_COLS):
        gate = jnp.dot(xn_ref[...], wgu_ref[:, c * FFN_COLS:(c + 1) * FFN_COLS],
                       preferred_element_type=F32)
        up = jnp.dot(xn_ref[...], wgu_ref[:, ff + c * FFN_COLS:ff + (c + 1) * FFN_COLS],
                     preferred_element_type=F32)
        act_ref[:, c * FFN_COLS:(c + 1) * FFN_COLS] = (gate * _sigmoid(gate) * up).astype(BF16)
    x2 = out_ref[...] + jnp.dot(act_ref[...], wd_ref[...], preferred_element_type=F32)
    if final_norm:
        x2 = _rmsnorm(x2, gfin_ref[...])
    out_ref[...] = x2


def _post(x, z, wo, g, wgu, wd, gfin, tm, final_norm):
    t, d = x.shape
    ff = wd.shape[0]
    assert ff % FFN_COLS == 0
    tile = pl.BlockSpec((tm, d), lambda i: (i, 0))
    return pl.pallas_call(
        functools.partial(_post_kernel, final_norm=final_norm),
        grid=(t // tm,),
        in_specs=[tile, tile, _resident((d, d)), _resident((1, d)),
                  _resident((d, 2 * ff)), _resident((ff, d)), _resident((1, d))],
        out_specs=tile,
        out_shape=jax.ShapeDtypeStruct((t, d), F32),
        scratch_shapes=[pltpu.VMEM((tm, d), BF16), pltpu.VMEM((tm, ff), BF16)],
        compiler_params=_params(1),
        name="outproj_ffn_final" if final_norm else "outproj_ffn",
    )(x, z, wo, g, wgu, wd, gfin)


def _conv_taps(u, prev1, prev2, taps_ref):
    return taps_ref[0:1, :] * prev2 + taps_ref[1:2, :] * prev1 + taps_ref[2:3, :] * u


def _conv_proj(x_ref, g_ref, w_ref):
    d = x_ref.shape[1]
    xn = _rmsnorm(x_ref[...], g_ref[...]).astype(BF16)
    part = lambda c: jnp.dot(xn, w_ref[:, c * d:(c + 1) * d], preferred_element_type=F32)
    return part(0), part(1) * part(2)


def _conv_long_kernel(x_ref, g_ref, w_ref, taps_ref, st_ref, z_ref, st_out_ref, carry_ref):
    tm = x_ref.shape[0]

    @pl.when(pl.program_id(1) == 0)
    def _():
        carry_ref[...] = st_ref[0]

    bg, u = _conv_proj(x_ref, g_ref, w_ref)
    row = lax.broadcasted_iota(jnp.int32, (tm, 1), 0)
    c0, c1 = carry_ref[0:1, :], carry_ref[1:2, :]
    prev1 = jnp.where(row == 0, c1, pltpu.roll(u, 1, 0))
    prev2 = jnp.where(row == 0, c0, jnp.where(row == 1, c1, pltpu.roll(u, 2, 0)))
    z_ref[...] = (bg * _conv_taps(u, prev1, prev2, taps_ref)).astype(BF16)
    tail = u[tm - (CONV_W - 1):tm, :]
    carry_ref[...] = tail
    st_out_ref[0] = tail


def _conv_short_kernel(x_ref, g_ref, w_ref, taps_ref, st_ref, z_ref, st_out_ref, *, seq):
    tm, d = x_ref.shape
    nseq = tm // seq
    bg, u = _conv_proj(x_ref, g_ref, w_ref)
    pos = lax.broadcasted_iota(jnp.int32, (1, seq, 1), 1)
    s0, s1 = st_ref[:, 0:1, :], st_ref[:, 1:2, :]
    u3 = u.reshape(nseq, seq, d)
    r1 = pltpu.roll(u, 1, 0).reshape(nseq, seq, d)
    r2 = pltpu.roll(u, 2, 0).reshape(nseq, seq, d)
    prev1 = jnp.where(pos == 0, s1, r1)
    prev2 = jnp.where(pos == 0, s0, jnp.where(pos == 1, s1, r2))
    conv = (taps_ref[0:1, :].reshape(1, 1, d) * prev2 + taps_ref[1:2, :].reshape(1, 1, d) * prev1
            + taps_ref[2:3, :].reshape(1, 1, d) * u3)
    z_ref[...] = (bg * conv.reshape(tm, d)).astype(BF16)
    st_out_ref[...] = u3[:, seq - (CONV_W - 1):seq, :]


def _conv(x, g, w, taps, state, batch, seq, tm):
    t, d = x.shape
    keep = CONV_W - 1
    out_shape = [jax.ShapeDtypeStruct((t, d), BF16), jax.ShapeDtypeStruct((batch, keep, d), F32)]
    weights = [_resident((1, d)), _resident((d, 3 * d)), _resident((SUBLANES, d))]
    if seq >= tm:
        nt = seq // tm
        st_spec = pl.BlockSpec((1, keep, d), lambda b, j: (b, 0, 0))
        tile = pl.BlockSpec((tm, d), lambda b, j: (b * nt + j, 0))
        return pl.pallas_call(
            _conv_long_kernel,
            grid=(batch, nt),
            in_specs=[tile] + weights + [st_spec],
            out_specs=[tile, st_spec],
            out_shape=out_shape,
            scratch_shapes=[pltpu.VMEM((keep, d), F32)],
            compiler_params=_params(2),
            name="gated_conv_long",
        )(x, g, w, taps, state)
    nseq = tm // seq
    st_spec = pl.BlockSpec((nseq, keep, d), lambda i: (i, 0, 0))
    tile = pl.BlockSpec((tm, d), lambda i: (i, 0))
    return pl.pallas_call(
        functools.partial(_conv_short_kernel, seq=seq),
        grid=(t // tm,),
        in_specs=[tile] + weights + [st_spec],
        out_specs=[tile, st_spec],
        out_shape=out_shape,
        compiler_params=_params(1),
        name="gated_conv_short",
    )(x, g, w, taps, state)


def _token_tile(t):
    return min(t, 512)


def _trunk(x, c0, n0, m0, conv0, wts, sample_bt):
    batch, seq, d = x.shape
    t = batch * seq
    tm = _token_tile(t)
    x2d = x.reshape(t, d)
    proj, gates = _inproj(x2d, wts["norm_mix0"], wts["w_in"], wts["w_gate"], wts["b_gate"], tm)
    if c0 is None:
        hg, c_fin, n_pad, m_pad = _scan_prompt(proj, gates, wts["g_head"], batch, seq)
        n_fin, m_fin = n_pad[:, :N_HEADS, :], m_pad[:, 0, :N_HEADS]
    else:
        m0_pad = jnp.pad(m0, ((0, 0), (0, LANES - N_HEADS)))
        hg, c_fin, n_fin, m_pad = _scan_sample(proj, gates, wts["g_head"], c0, n0, m0_pad, seq, sample_bt)
        m_fin = m_pad[:, :N_HEADS]
    x1 = _post(x2d, hg, wts["w_out0"], wts["norm_ffn0"], wts["w_gu0"], wts["w_down0"],
               wts["norm_final"], tm, final_norm=False)
    z, conv_fin = _conv(x1, wts["norm_mix1"], wts["w_conv_in"], wts["taps"], conv0, batch, seq, tm)
    y = _post(x1, z, wts["w_conv_out"], wts["norm_ffn1"], wts["w_gu1"], wts["w_down1"],
              wts["norm_final"], tm, final_norm=True)
    return (y.reshape(batch, seq, d), c_fin[None], n_fin[None], m_fin[None], conv_fin[None])


def kernel(x_prompt, x_sample, state_mlstm_C, state_mlstm_n, state_mlstm_m, state_conv, norm_mix, norm_ffn, norm_final, w_mlstm_in, b_mlstm_gate, mlstm_head_norm, w_mlstm_out, w_conv_in, w_conv_taps, w_conv_out, w_ffn_gate_up, w_ffn_down):
    d = x_prompt.shape[-1]
    dk = d // N_HEADS
    assert norm_mix.shape[0] == 2 and w_mlstm_in.shape[0] == 1 and w_conv_in.shape[0] == 1
    assert w_conv_taps.shape[1] == CONV_W and 2 * N_HEADS <= LANES
    row = lambda v: v.reshape(1, -1).astype(F32)
    tile_pad = lambda m: jnp.pad(m, ((0, 0), (0, LANES - N_HEADS)))
    w_in = w_mlstm_in[0]
    q_scale = jnp.concatenate([jnp.full((d,), dk ** -0.5, F32), jnp.ones((3 * d,), F32)])
    wts = {
        "norm_mix0": row(norm_mix[0]), "norm_mix1": row(norm_mix[1]),
        "norm_ffn0": row(norm_ffn[0]), "norm_ffn1": row(norm_ffn[1]),
        "norm_final": row(norm_final), "g_head": row(mlstm_head_norm[0]),
        "w_in": (w_in[:, :4 * d] * q_scale).astype(BF16),
        "w_gate": jnp.concatenate([tile_pad(w_in[:, 4 * d:4 * d + N_HEADS]),
                                   tile_pad(w_in[:, 4 * d + N_HEADS:])], axis=1).astype(BF16),
        "b_gate": jnp.concatenate([tile_pad(b_mlstm_gate[:, :N_HEADS]),
                                   tile_pad(b_mlstm_gate[:, N_HEADS:])], axis=1).astype(F32),
        "w_out0": w_mlstm_out[0].astype(BF16),
        "w_conv_in": w_conv_in[0].astype(BF16),
        "w_conv_out": w_conv_out[0].astype(BF16),
        "taps": jnp.pad(w_conv_taps[0], ((0, SUBLANES - CONV_W), (0, 0))).astype(F32),
        "w_gu0": w_ffn_gate_up[0].astype(BF16), "w_gu1": w_ffn_gate_up[1].astype(BF16),
        "w_down0": w_ffn_down[0].astype(BF16), "w_down1": w_ffn_down[1].astype(BF16),
    }
    bp = x_prompt.shape[0]
    zero_conv = jnp.zeros((bp, CONV_W - 1, d), F32)
    y_p, c_p, n_p, m_p, conv_p = _trunk(x_prompt, None, None, None, zero_conv, wts, None)
    y_s, c_s, n_s, m_s, conv_s = _trunk(
        x_sample, state_mlstm_C[0], state_mlstm_n[0], state_mlstm_m[0], state_conv[0], wts, 4)
    return (y_p, y_s, c_p, n_p, m_p, conv_p, c_s, n_s, m_s, conv_s)
```

```python
import functools

import jax
import jax.numpy as jnp
from jax import lax
from jax.experimental import pallas as pl
from jax.experimental.pallas import tpu as pltpu

EPS = 1e-6
N_HEADS = 4
MLSTM_CHUNK = 128
CONV_W = 3
LANES = 128
SUBLANES = 8
GATE_COLS = 2 * LANES
FFN_COLS = 256
VMEM_LIMIT = 56 * 1024 * 1024

F32 = jnp.float32
BF16 = jnp.bfloat16
_NT = (((1,), (1,)), ((), ()))
_TN = (((0,), (0,)), ((), ()))


def _params(n_grid):
    return pltpu.CompilerParams(
        dimension_semantics=("arbitrary",) * n_grid, vmem_limit_bytes=VMEM_LIMIT)


def _resident(shape):
    zeros = (0,) * len(shape)
    return pl.BlockSpec(shape, lambda *_: zeros, pipeline_mode=pl.Buffered(1))


def _rmsnorm(x, g):
    return x * lax.rsqrt(jnp.mean(x * x, axis=-1, keepdims=True) + EPS) * g


def _sigmoid(x):
    return 1.0 / (1.0 + jnp.exp(-x))


def _log_sigmoid(x):
    return jnp.minimum(x, 0.0) - jnp.log1p(jnp.exp(-jnp.abs(x)))


def _prefix(x, axis, combine, identity):
    n = x.shape[axis]
    idx = lax.broadcasted_iota(jnp.int32, x.shape, axis)
    shift = 1
    while shift < n:
        x = combine(x, jnp.where(idx >= shift, pltpu.roll(x, shift, axis), identity))
        shift *= 2
    return x


def _cumsum(x, axis):
    return _prefix(x, axis, jnp.add, 0.0)


INPROJ_COLS = 256


def _inproj_pieces(x_ref, g_ref, w_ref, wg_ref, bg_ref, proj_ref, gate_ref):
    d = x_ref.shape[1]
    n = w_ref.shape[1]
    xn = _rmsnorm(x_ref[...], g_ref[...]).astype(BF16)
    yield
    for c0 in range(0, n, INPROJ_COLS):
        cols = slice(c0, c0 + INPROJ_COLS)
        y = jnp.dot(xn, w_ref[:, cols], preferred_element_type=F32)
        if c0 >= n - d:
            y = _sigmoid(y)
        proj_ref[:, cols] = y.astype(BF16)
        yield
    gate_ref[...] = jnp.dot(xn, wg_ref[...], preferred_element_type=F32) + bg_ref[...]
    yield


def _n_inproj_pieces(n):
    return n // INPROJ_COLS + 2


def _inproj_kernel(*refs):
    for _ in _inproj_pieces(*refs):
        pass


def _zip_work(primary, secondary, n_primary, n_secondary):
    next(secondary, None)
    done = 1
    for k, _ in enumerate(primary, start=1):
        while done * n_primary < k * n_secondary:
            next(secondary, None)
            done += 1
    for _ in secondary:
        pass


def _inproj(x, g, w, wg, bg, tm):
    t, d = x.shape
    n = w.shape[1]
    return pl.pallas_call(
        _inproj_kernel,
        grid=(t // tm,),
        in_specs=[pl.BlockSpec((tm, d), lambda i: (i, 0)),
                  _resident((1, d)), _resident((d, n)), _resident((d, GATE_COLS)), _resident((1, GATE_COLS))],
        out_specs=[pl.BlockSpec((tm, n), lambda i: (i, 0)),
                   pl.BlockSpec((tm, GATE_COLS), lambda i: (i, 0))],
        out_shape=[jax.ShapeDtypeStruct((t, n), BF16), jax.ShapeDtypeStruct((t, GATE_COLS), F32)],
        compiler_params=_params(1),
        name="mlstm_inproj",
    )(x, g, w, wg, bg)


def _mlstm_head(q, k, v, li_c, li_r, b_c, b_r, m_prev, c_mat, n_row, causal):
    seq = q.shape[0]
    qf, kf, vf = q.astype(F32), k.astype(F32), v.astype(F32)
    b_last = b_c[seq - 1:seq, :]
    dm = jnp.where(causal, b_c - b_r + li_r, -jnp.inf)
    inter = b_c + m_prev
    m_eff = jnp.maximum(inter, jnp.max(dm, axis=1, keepdims=True))
    w_inter = jnp.exp(inter - m_eff)
    s = lax.dot_general(q, k, _NT, preferred_element_type=F32) * jnp.exp(dm - m_eff)
    qc = lax.dot_general(q, c_mat.astype(q.dtype), _NT, preferred_element_type=F32)
    num = w_inter * qc + jnp.dot(s.astype(v.dtype), v, preferred_element_type=F32)
    den = w_inter * jnp.sum(qf * n_row, axis=1, keepdims=True) + jnp.sum(s, axis=1, keepdims=True)
    h = num * (1.0 / jnp.maximum(jnp.abs(den), jnp.exp(-m_eff)))
    g = b_last - b_c + li_c
    m_new = jnp.maximum(b_last + m_prev, jnp.max(g, axis=0, keepdims=True))
    decay = jnp.exp(b_last + m_prev - m_new)
    wg = jnp.exp(g - m_new)
    upd = lax.dot_general((vf * wg).astype(v.dtype), k, _TN, preferred_element_type=F32)
    c_new = decay * c_mat + upd
    n_new = decay * n_row + jnp.sum(kf * wg, axis=0, keepdims=True)
    return h, c_new, n_new, m_new


def _gated_head_out(h, g_head, sig_o):
    hn = h * lax.rsqrt(jnp.mean(h * h, axis=-1, keepdims=True) + EPS)
    return hn * g_head * sig_o


def _per_row(rep, width):
    return jnp.concatenate([rep] * (width // LANES), axis=1)


def _scan_chunk(proj_ref, gate_ref, gh_ref, hg_ref, c_ref, n_ref, m_ref, r0):
    seq = MLSTM_CHUNK
    d = gh_ref.shape[1]
    dk = d // N_HEADS
    assert seq == LANES
    rows = slice(r0, r0 + seq)

    lane = lax.broadcasted_iota(jnp.int32, (1, LANES), 1)
    li = gate_ref[rows, 0:LANES]
    b = _prefix(_log_sigmoid(gate_ref[rows, LANES:GATE_COLS]), 0, jnp.add, 0.0)
    a = li - b
    m_prev = m_ref[0, 0:1, :]
    neg_mx = -jnp.maximum(m_prev, _prefix(a, 0, jnp.maximum, -jnp.inf))
    exp_neg_meff = jnp.exp(neg_mx - b)
    b_last = b[seq - 1:seq, :]
    g = b_last - b + li
    m_new = jnp.maximum(b_last + m_prev, jnp.max(g, axis=0, keepdims=True))
    decay_row = jnp.exp(b_last + m_prev - m_new)
    wg_t = jnp.exp(g - m_new).T
    a_t = a.T
    m_ref[0, 0:1, :] = jnp.where(lane < N_HEADS, m_new, 0.0)

    causal = (lax.broadcasted_iota(jnp.int32, (seq, seq), 0)
              >= lax.broadcasted_iota(jnp.int32, (seq, seq), 1))
    ones_l = jnp.ones((seq, LANES), BF16)
    ones_v = jnp.ones((dk, LANES), BF16)
    wg16 = wg_t[0:2 * SUBLANES, :].astype(BF16)
    nums = []
    den = jnp.zeros((seq, LANES), F32)
    yield
    for h in range(N_HEADS):
        q, k, v = (proj_ref[rows, j * d + h * dk:j * d + (h + 1) * dk] for j in range(3))
        c_mat, n_row = c_ref[0, h], n_ref[0, h:h + 1, :]
        neg_mx_rep = jnp.broadcast_to(neg_mx[:, h:h + 1], (seq, LANES))
        p = jnp.exp(jnp.where(causal, neg_mx_rep + a_t[h:h + 1, :], -jnp.inf))
        w_rep = jnp.exp(neg_mx_rep + m_prev[:, h:h + 1])
        rhs = jnp.concatenate(
            [k, c_mat.astype(BF16), jnp.broadcast_to(n_row.astype(BF16), (LANES, dk))], axis=0)
        qx = lax.dot_general(q, rhs, _NT, preferred_element_type=F32)
        s = (qx[:, :seq] * p).astype(BF16)
        sv = jnp.dot(s, jnp.concatenate([v, ones_l], axis=1), preferred_element_type=F32)
        nums.append(_per_row(w_rep, dk) * qx[:, seq:seq + dk] + sv[:, :dk])
        den = jnp.where(lane == h, w_rep * qx[:, seq + dk:] + sv[:, dk:], den)
        vtw = (v.T.astype(F32) * wg_t[h:h + 1, :]).astype(BF16)
        upd = jnp.dot(jnp.concatenate([vtw, wg16], axis=0), k, preferred_element_type=F32)
        decay = decay_row[:, h:h + 1]
        c_ref[0, h] = decay * c_mat + upd[:dk]
        n_ref[0, h:h + 1, :] = decay * n_row + upd[dk + h:dk + h + 1, :]
        yield
    recip = 1.0 / jnp.maximum(jnp.abs(den), exp_neg_meff)
    yield
    for h in range(N_HEADS):
        cols = slice(h * dk, (h + 1) * dk)
        hh = nums[h] * _per_row(jnp.broadcast_to(recip[:, h:h + 1], (seq, LANES)), dk)
        ssq_rep = jnp.dot((hh * hh).astype(BF16), ones_v, preferred_element_type=F32)
        hn = hh * _per_row(lax.rsqrt(ssq_rep * (1.0 / dk) + EPS), dk)
        sig_o = proj_ref[rows, 3 * d + h * dk:3 * d + (h + 1) * dk].astype(F32)
        hg_ref[rows, cols] = (hn * gh_ref[:, cols] * sig_o).astype(BF16)
        yield


SCAN_CHUNK_PIECES = 2 * N_HEADS + 2


def _mlstm_long_kernel(x_ref, g_ref, w_ref, wg_ref, bg_ref, gh_ref,
                       hg_ref, c_ref, n_ref, m_ref,
                       proj_a, gate_a, proj_b, gate_b, *, tiles_per_seq):
    i = pl.program_id(0)
    tm = x_ref.shape[0]

    @pl.when(i == 0)
    def _():
        proj_b[...] = jnp.zeros_like(proj_b)
        gate_b[...] = jnp.zeros_like(gate_b)

    @pl.when(lax.rem(jnp.maximum(i - 1, 0), tiles_per_seq) == 0)
    def _():
        c_ref[...] = jnp.zeros_like(c_ref)
        n_ref[...] = jnp.zeros_like(n_ref)
        m_ref[...] = jnp.zeros_like(m_ref)

    n_chunks = tm // MLSTM_CHUNK

    def scan_tile(proj_r, gate_r):
        for c in range(n_chunks):
            yield from _scan_chunk(proj_r, gate_r, gh_ref, hg_ref, c_ref, n_ref, m_ref, c * MLSTM_CHUNK)

    def step(proj_w, gate_w, proj_r, gate_r):
        _zip_work(scan_tile(proj_r, gate_r),
                  _inproj_pieces(x_ref, g_ref, w_ref, wg_ref, bg_ref, proj_w, gate_w),
                  n_chunks * SCAN_CHUNK_PIECES, _n_inproj_pieces(w_ref.shape[1]))

    @pl.when(lax.rem(i, 2) == 0)
    def _():
        step(proj_a, gate_a, proj_b, gate_b)

    @pl.when(lax.rem(i, 2) == 1)
    def _():
        step(proj_b, gate_b, proj_a, gate_a)


def _mlstm_long(x, g, w, wg, bg, g_head, batch, seq, tm):
    t, d = x.shape
    n = w.shape[1]
    dk = d // N_HEADS
    n_tiles = t // tm
    tiles_per_seq = seq // tm
    assert seq % tm == 0 and tm % MLSTM_CHUNK == 0
    prev = lambda i: jnp.maximum(i - 1, 0)
    return pl.pallas_call(
        functools.partial(_mlstm_long_kernel, tiles_per_seq=tiles_per_seq),
        grid=(n_tiles + 1,),
        in_specs=[pl.BlockSpec((tm, d), lambda i: (jnp.minimum(i, n_tiles - 1), 0)),
                  _resident((1, d)), _resident((d, n)), _resident((d, GATE_COLS)),
                  _resident((1, GATE_COLS)), _resident((1, d))],
        out_specs=[pl.BlockSpec((tm, d), lambda i: (prev(i), 0)),
                   pl.BlockSpec((1, N_HEADS, dk, dk), lambda i: (prev(i) // tiles_per_seq, 0, 0, 0)),
                   pl.BlockSpec((1, SUBLANES, dk), lambda i: (prev(i) // tiles_per_seq, 0, 0)),
                   pl.BlockSpec((1, SUBLANES, LANES), lambda i: (prev(i) // tiles_per_seq, 0, 0))],
        out_shape=[jax.ShapeDtypeStruct((t, d), BF16),
                   jax.ShapeDtypeStruct((batch, N_HEADS, dk, dk), F32),
                   jax.ShapeDtypeStruct((batch, SUBLANES, dk), F32),
                   jax.ShapeDtypeStruct((batch, SUBLANES, LANES), F32)],
        scratch_shapes=[pltpu.VMEM((tm, n), BF16), pltpu.VMEM((tm, GATE_COLS), F32),
                        pltpu.VMEM((tm, n), BF16), pltpu.VMEM((tm, GATE_COLS), F32)],
        compiler_params=_params(1),
        name="mlstm_long",
    )(x, g, w, wg, bg, g_head)


def _scan_sample_kernel(q_ref, k_ref, v_ref, so_ref, gate_ref, gh_ref, c_in, n_in, m_in,
                        hg_ref, c_out, n_out, m_out, *, seq):
    bt = c_in.shape[0]
    dk = q_ref.shape[1] // N_HEADS
    step = pl.program_id(0)
    row = lax.broadcasted_iota(jnp.int32, (seq, seq), 0)
    col = lax.broadcasted_iota(jnp.int32, (seq, seq), 1)
    causal = row >= col
    diag = row == col
    lane = lax.broadcasted_iota(jnp.int32, (1, LANES), 1)
    to_row = lambda x_col: jnp.sum(jnp.where(diag, x_col, 0.0), axis=0, keepdims=True)

    qs, ks, vs = q_ref[...].astype(F32), k_ref[...].astype(F32), v_ref[...].astype(F32)
    sig_o = so_ref[...].astype(F32)
    for j in range(bt):
        rows = slice(j * seq, (j + 1) * seq)
        gates = gate_ref[rows, :]
        b_col = _cumsum(_log_sigmoid(gates), 0)
        m_prev_row = m_in[pl.ds(step * bt + j, 1), :]
        m_new_row = jnp.zeros((1, LANES), F32)
        for h in range(N_HEADS):
            cols = slice(h * dk, (h + 1) * dk)
            f = LANES + h
            li_c, b_c = gates[:, h:h + 1], b_col[:, f:f + 1]
            hh, c_new, n_new, m_new = _mlstm_head(
                qs[rows, cols], ks[rows, cols], vs[rows, cols],
                li_c, to_row(li_c), b_c, to_row(b_c),
                m_prev_row[:, h:h + 1], c_in[j, h], n_in[j, h:h + 1, :], causal)
            c_out[j, h] = c_new
            n_out[j, h:h + 1, :] = n_new
            m_new_row = jnp.where(lane == h, m_new, m_new_row)
            hg_ref[rows, cols] = _gated_head_out(hh, gh_ref[:, cols], sig_o[rows, cols])
        m_out[pl.ds(step * bt + j, 1), :] = m_new_row


def _scan_sample(proj, gates, g_head, c0, n0, m0, seq, bt):
    batch = c0.shape[0]
    d = g_head.shape[1]
    dk = d // N_HEADS
    tok = lambda j: pl.BlockSpec((bt * seq, d), lambda i: (i, j))
    c_spec = pl.BlockSpec((bt, N_HEADS, dk, dk), lambda i: (i, 0, 0, 0))
    n_spec = pl.BlockSpec((bt, N_HEADS, dk), lambda i: (i, 0, 0))
    m_spec = pl.BlockSpec((batch, LANES), lambda i: (0, 0))
    return pl.pallas_call(
        functools.partial(_scan_sample_kernel, seq=seq),
        grid=(batch // bt,),
        in_specs=[tok(0), tok(1), tok(2), tok(3),
                  pl.BlockSpec((bt * seq, GATE_COLS), lambda i: (i, 0)),
                  _resident((1, d)), c_spec, n_spec, m_spec],
        out_specs=[tok(0), c_spec, n_spec, m_spec],
        out_shape=[jax.ShapeDtypeStruct((batch * seq, d), F32),
                   jax.ShapeDtypeStruct(c0.shape, F32),
                   jax.ShapeDtypeStruct(n0.shape, F32),
                   jax.ShapeDtypeStruct((batch, LANES), F32)],
        compiler_params=_params(1),
        name="mlstm_scan_sample",
    )(proj, proj, proj, proj, gates, g_head, c0, n0, m0)


def _post_kernel(x_ref, z_ref, wo_ref, g_ref, wgu_ref, wd_ref, gfin_ref, out_ref,
                 xn_ref, act_ref, *, final_norm):
    x1 = x_ref[...] + jnp.dot(z_ref[...].astype(BF16), wo_ref[...], preferred_element_type=F32)
    out_ref[...] = x1
    xn_ref[...] = _rmsnorm(x1, g_ref[...]).astype(BF16)
    ff = wd_ref.shape[0]
    for c in range(ff // FFN_COLS):
        gate = jnp.dot(xn_ref[...], wgu_ref[:, c * FFN_COLS:(c + 1) * FFN_COLS],
                       preferred_element_type=F32)
        up = jnp.dot(xn_ref[...], wgu_ref[:, ff + c * FFN_COLS:ff + (c + 1) * FFN_COLS],
                     preferred_element_type=F32)
        act_ref[:, c * FFN_COLS:(c + 1) * FFN_COLS] = (gate * _sigmoid(gate) * up).astype(BF16)
    x2 = out_ref[...] + jnp.dot(act_ref[...], wd_ref[...], preferred_element_type=F32)
    if final_norm:
        x2 = _rmsnorm(x2, gfin_ref[...])
    out_ref[...] = x2


def _post(x, z, wo, g, wgu, wd, gfin, tm, final_norm):
    t, d = x.shape
    ff = wd.shape[0]
    assert ff % FFN_COLS == 0
    tile = pl.BlockSpec((tm, d), lambda i: (i, 0))
    return pl.pallas_call(
        functools.partial(_post_kernel, final_norm=final_norm),
        grid=(t // tm,),
        in_specs=[tile, tile, _resident((d, d)), _resident((1, d)),
                  _resident((d, 2 * ff)), _resident((ff, d)), _resident((1, d))],
        out_specs=tile,
        out_shape=jax.ShapeDtypeStruct((t, d), F32),
        scratch_shapes=[pltpu.VMEM((tm, d), BF16), pltpu.VMEM((tm, ff), BF16)],
        compiler_params=_params(1),
        name="outproj_ffn_final" if final_norm else "outproj_ffn",
    )(x, z, wo, g, wgu, wd, gfin)


def _conv_taps(u, prev1, prev2, taps_ref):
    return taps_ref[0:1, :] * prev2 + taps_ref[1:2, :] * prev1 + taps_ref[2:3, :] * u


def _conv_proj(x_ref, g_ref, w_ref):
    d = x_ref.shape[1]
    xn = _rmsnorm(x_ref[...], g_ref[...]).astype(BF16)
    part = lambda c: jnp.dot(xn, w_ref[:, c * d:(c + 1) * d], preferred_element_type=F32)
    return part(0), part(1) * part(2)


def _conv_long_kernel(x_ref, g_ref, w_ref, taps_ref, st_ref, z_ref, st_out_ref, carry_ref):
    tm = x_ref.shape[0]

    @pl.when(pl.program_id(1) == 0)
    def _():
        carry_ref[...] = st_ref[0]

    bg, u = _conv_proj(x_ref, g_ref, w_ref)
    row = lax.broadcasted_iota(jnp.int32, (tm, 1), 0)
    c0, c1 = carry_ref[0:1, :], carry_ref[1:2, :]
    prev1 = jnp.where(row == 0, c1, pltpu.roll(u, 1, 0))
    prev2 = jnp.where(row == 0, c0, jnp.where(row == 1, c1, pltpu.roll(u, 2, 0)))
    z_ref[...] = (bg * _conv_taps(u, prev1, prev2, taps_ref)).astype(BF16)
    tail = u[tm - (CONV_W - 1):tm, :]
    carry_ref[...] = tail
    st_out_ref[0] = tail


def _conv_short_kernel(x_ref, g_ref, w_ref, taps_ref, st_ref, z_ref, st_out_ref, *, seq):
    tm, d = x_ref.shape
    nseq = tm // seq
    bg, u = _conv_proj(x_ref, g_ref, w_ref)
    pos = lax.broadcasted_iota(jnp.int32, (1, seq, 1), 1)
    s0, s1 = st_ref[:, 0:1, :], st_ref[:, 1:2, :]
    u3 = u.reshape(nseq, seq, d)
    r1 = pltpu.roll(u, 1, 0).reshape(nseq, seq, d)
    r2 = pltpu.roll(u, 2, 0).reshape(nseq, seq, d)
    prev1 = jnp.where(pos == 0, s1, r1)
    prev2 = jnp.where(pos == 0, s0, jnp.where(pos == 1, s1, r2))
    conv = (taps_ref[0:1, :].reshape(1, 1, d) * prev2 + taps_ref[1:2, :].reshape(1, 1, d) * prev1
            + taps_ref[2:3, :].reshape(1, 1, d) * u3)
    z_ref[...] = (bg * conv.reshape(tm, d)).astype(BF16)
    st_out_ref[...] = u3[:, seq - (CONV_W - 1):seq, :]


def _conv(x, g, w, taps, state, batch, seq, tm):
    t, d = x.shape
    keep = CONV_W - 1
    out_shape = [jax.ShapeDtypeStruct((t, d), BF16), jax.ShapeDtypeStruct((batch, keep, d), F32)]
    weights = [_resident((1, d)), _resident((d, 3 * d)), _resident((SUBLANES, d))]
    if seq >= tm:
        nt = seq // tm
        st_spec = pl.BlockSpec((1, keep, d), lambda b, j: (b, 0, 0))
        tile = pl.BlockSpec((tm, d), lambda b, j: (b * nt + j, 0))
        return pl.pallas_call(
            _conv_long_kernel,
            grid=(batch, nt),
            in_specs=[tile] + weights + [st_spec],
            out_specs=[tile, st_spec],
            out_shape=out_shape,
            scratch_shapes=[pltpu.VMEM((keep, d), F32)],
            compiler_params=_params(2),
            name="gated_conv_long",
        )(x, g, w, taps, state)
    nseq = tm // seq
    st_spec = pl.BlockSpec((nseq, keep, d), lambda i: (i, 0, 0))
    tile = pl.BlockSpec((tm, d), lambda i: (i, 0))
    return pl.pallas_call(
        functools.partial(_conv_short_kernel, seq=seq),
        grid=(t // tm,),
        in_specs=[tile] + weights + [st_spec],
        out_specs=[tile, st_spec],
        out_shape=out_shape,
        compiler_params=_params(1),
        name="gated_conv_short",
    )(x, g, w, taps, state)


def _token_tile(t):
    return min(t, 512)


def _trunk(x, c0, n0, m0, conv0, wts, sample_bt):
    batch, seq, d = x.shape
    t = batch * seq
    tm = _token_tile(t)
    x2d = x.reshape(t, d)
    if c0 is None:
        hg, c_fin, n_pad, m_pad = _mlstm_long(x2d, wts["norm_mix0"], wts["w_in"], wts["w_gate"],
                                              wts["b_gate"], wts["g_head"], batch, seq, tm)
        n_fin, m_fin = n_pad[:, :N_HEADS, :], m_pad[:, 0, :N_HEADS]
    else:
        proj, gates = _inproj(x2d, wts["norm_mix0"], wts["w_in"], wts["w_gate"], wts["b_gate"], tm)
        m0_pad = jnp.pad(m0, ((0, 0), (0, LANES - N_HEADS)))
        hg, c_fin, n_fin, m_pad = _scan_sample(proj, gates, wts["g_head"], c0, n0, m0_pad, seq, sample_bt)
        m_fin = m_pad[:, :N_HEADS]
    x1 = _post(x2d, hg, wts["w_out0"], wts["norm_ffn0"], wts["w_gu0"], wts["w_down0"],
               wts["norm_final"], tm, final_norm=False)
    z, conv_fin = _conv(x1, wts["norm_mix1"], wts["w_conv_in"], wts["taps"], conv0, batch, seq, tm)
    y = _post(x1, z, wts["w_conv_out"], wts["norm_ffn1"], wts["w_gu1"], wts["w_down1"],
              wts["norm_final"], tm, final_norm=True)
    return (y.reshape(batch, seq, d), c_fin[None], n_fin[None], m_fin[None], conv_fin[None])


def kernel(x_prompt, x_sample, state_mlstm_C, state_mlstm_n, state_mlstm_m, state_conv, norm_mix, norm_ffn, norm_final, w_mlstm_in, b_mlstm_gate, mlstm_head_norm, w_mlstm_out, w_conv_in, w_conv_taps, w_conv_out, w_ffn_gate_up, w_ffn_down):
    d = x_prompt.shape[-1]
    dk = d // N_HEADS
    assert norm_mix.shape[0] == 2 and w_mlstm_in.shape[0] == 1 and w_conv_in.shape[0] == 1
    assert w_conv_taps.shape[1] == CONV_W and 2 * N_HEADS <= LANES
    row = lambda v: v.reshape(1, -1).astype(F32)
    tile_pad = lambda m: jnp.pad(m, ((0, 0), (0, LANES - N_HEADS)))
    w_in = w_mlstm_in[0]
    q_scale = jnp.concatenate([jnp.full((d,), dk ** -0.5, F32), jnp.ones((3 * d,), F32)])
    wts = {
        "norm_mix0": row(norm_mix[0]), "norm_mix1": row(norm_mix[1]),
        "norm_ffn0": row(norm_ffn[0]), "norm_ffn1": row(norm_ffn[1]),
        "norm_final": row(norm_final), "g_head": row(mlstm_head_norm[0]),
        "w_in": (w_in[:, :4 * d] * q_scale).astype(BF16),
        "w_gate": jnp.concatenate([tile_pad(w_in[:, 4 * d:4 * d + N_HEADS]),
                                   tile_pad(w_in[:, 4 * d + N_HEADS:])], axis=1).astype(BF16),
        "b_gate": jnp.concatenate([tile_pad(b_mlstm_gate[:, :N_HEADS]),
                                   tile_pad(b_mlstm_gate[:, N_HEADS:])], axis=1).astype(F32),
        "w_out0": w_mlstm_out[0].astype(BF16),
        "w_conv_in": w_conv_in[0].astype(BF16),
        "w_conv_out": w_conv_out[0].astype(BF16),
        "taps": jnp.pad(w_conv_taps[0], ((0, SUBLANES - CONV_W), (0, 0))).astype(F32),
        "w_gu0": w_ffn_gate_up[0].astype(BF16), "w_gu1": w_ffn_gate_up[1].astype(BF16),
        "w_down0": w_ffn_down[0].astype(BF16), "w_down1": w_ffn_down[1].astype(BF16),
    }
    bp = x_prompt.shape[0]
    zero_conv = jnp.zeros((bp, CONV_W - 1, d), F32)
    y_p, c_p, n_p, m_p, conv_p = _trunk(x_prompt, None, None, None, zero_conv, wts, None)
    y_s, c_s, n_s, m_s, conv_s = _trunk(
        x_sample, state_mlstm_C[0], state_mlstm_n[0], state_mlstm_m[0], state_conv[0], wts, 4)
    return (y_p, y_s, c_p, n_p, m_p, conv_p, c_s, n_s, m_s, conv_s)
```

```python
import functools

import jax
import jax.numpy as jnp
from jax import lax
from jax.experimental import pallas as pl
from jax.experimental.pallas import tpu as pltpu

EPS = 1e-6
N_HEADS = 4
MLSTM_CHUNK = 128
CONV_W = 3
LANES = 128
SUBLANES = 8
GATE_COLS = 2 * LANES
FFN_COLS = 256
VMEM_LIMIT = 56 * 1024 * 1024

F32 = jnp.float32
BF16 = jnp.bfloat16
_NT = (((1,), (1,)), ((), ()))
_TN = (((0,), (0,)), ((), ()))


def _params(n_grid):
    return pltpu.CompilerParams(
        dimension_semantics=("arbitrary",) * n_grid, vmem_limit_bytes=VMEM_LIMIT)


def _resident(shape):
    zeros = (0,) * len(shape)
    return pl.BlockSpec(shape, lambda *_: zeros, pipeline_mode=pl.Buffered(1))


def _rmsnorm(x, g):
    return x * lax.rsqrt(jnp.mean(x * x, axis=-1, keepdims=True) + EPS) * g


def _sigmoid(x):
    return 1.0 / (1.0 + jnp.exp(-x))


def _log_sigmoid(x):
    return jnp.minimum(x, 0.0) - jnp.log1p(jnp.exp(-jnp.abs(x)))


def _prefix(x, axis, combine, identity):
    n = x.shape[axis]
    idx = lax.broadcasted_iota(jnp.int32, x.shape, axis)
    shift = 1
    while shift < n:
        x = combine(x, jnp.where(idx >= shift, pltpu.roll(x, shift, axis), identity))
        shift *= 2
    return x


def _cumsum(x, axis):
    return _prefix(x, axis, jnp.add, 0.0)


INPROJ_COLS = 256


def _inproj_pieces(x_ref, g_ref, w_ref, wg_ref, bg_ref, proj_ref, gate_ref):
    d = x_ref.shape[1]
    n = w_ref.shape[0]
    xn = _rmsnorm(x_ref[...], g_ref[...]).astype(BF16)
    yield
    for c0 in range(0, n, INPROJ_COLS):
        cols = slice(c0, c0 + INPROJ_COLS)
        y = lax.dot_general(xn, w_ref[cols, :], _NT, preferred_element_type=F32)
        if c0 >= n - d:
            y = _sigmoid(y)
        proj_ref[:, cols] = y.astype(BF16)
        yield
    gate_ref[...] = lax.dot_general(xn, wg_ref[...], _NT, preferred_element_type=F32) + bg_ref[...]
    yield


def _n_inproj_pieces(n):
    return n // INPROJ_COLS + 2


def _inproj_kernel(*refs):
    for _ in _inproj_pieces(*refs):
        pass


def _zip_work(primary, secondary, n_primary, n_secondary):
    next(secondary, None)
    done = 1
    for k, _ in enumerate(primary, start=1):
        while done * n_primary < k * n_secondary:
            next(secondary, None)
            done += 1
    for _ in secondary:
        pass


def _inproj(x, g, w, wg, bg, tm):
    t, d = x.shape
    n = w.shape[0]
    return pl.pallas_call(
        _inproj_kernel,
        grid=(t // tm,),
        in_specs=[pl.BlockSpec((tm, d), lambda i: (i, 0)),
                  _resident((1, d)), _resident((n, d)), _resident((GATE_COLS, d)), _resident((1, GATE_COLS))],
        out_specs=[pl.BlockSpec((tm, n), lambda i: (i, 0)),
                   pl.BlockSpec((tm, GATE_COLS), lambda i: (i, 0))],
        out_shape=[jax.ShapeDtypeStruct((t, n), BF16), jax.ShapeDtypeStruct((t, GATE_COLS), F32)],
        compiler_params=_params(1),
        name="mlstm_inproj",
    )(x, g, w, wg, bg)


def _gated_head_out(h, g_head, sig_o):
    hn = h * lax.rsqrt(jnp.mean(h * h, axis=-1, keepdims=True) + EPS)
    return hn * g_head * sig_o


def _per_row(rep, width):
    return jnp.concatenate([rep] * (width // LANES), axis=1)


def _scan_chunk(proj_ref, gate_ref, gh_ref, hg_ref, c_ref, n_ref, m_ref, r0):
    seq = MLSTM_CHUNK
    d = gh_ref.shape[1]
    dk = d // N_HEADS
    assert seq == LANES
    rows = slice(r0, r0 + seq)

    lane = lax.broadcasted_iota(jnp.int32, (1, LANES), 1)
    li = gate_ref[rows, 0:LANES]
    b = _prefix(_log_sigmoid(gate_ref[rows, LANES:GATE_COLS]), 0, jnp.add, 0.0)
    a = li - b
    m_prev = m_ref[0, 0:1, :]
    neg_mx = -jnp.maximum(m_prev, _prefix(a, 0, jnp.maximum, -jnp.inf))
    exp_neg_meff = jnp.exp(neg_mx - b)
    b_last = b[seq - 1:seq, :]
    g = b_last - b + li
    m_new = jnp.maximum(b_last + m_prev, jnp.max(g, axis=0, keepdims=True))
    decay_row = jnp.exp(b_last + m_prev - m_new)
    wg_t = jnp.exp(g - m_new).T
    a_t = a.T
    m_ref[0, 0:1, :] = jnp.where(lane < N_HEADS, m_new, 0.0)

    causal = (lax.broadcasted_iota(jnp.int32, (seq, seq), 0)
              >= lax.broadcasted_iota(jnp.int32, (seq, seq), 1))
    ones_l = jnp.ones((seq, LANES), BF16)
    ones_v = jnp.ones((dk, LANES), BF16)
    wg16 = wg_t[0:2 * SUBLANES, :].astype(BF16)
    nums = []
    den = jnp.zeros((seq, LANES), F32)
    yield
    for h in range(N_HEADS):
        q, k, v = (proj_ref[rows, j * d + h * dk:j * d + (h + 1) * dk] for j in range(3))
        c_mat, n_row = c_ref[0, h], n_ref[0, h:h + 1, :]
        neg_mx_rep = jnp.broadcast_to(neg_mx[:, h:h + 1], (seq, LANES))
        p = jnp.exp(jnp.where(causal, neg_mx_rep + a_t[h:h + 1, :], -jnp.inf))
        w_rep = jnp.exp(neg_mx_rep + m_prev[:, h:h + 1])
        rhs = jnp.concatenate(
            [k, c_mat.astype(BF16), jnp.broadcast_to(n_row.astype(BF16), (LANES, dk))], axis=0)
        qx = lax.dot_general(q, rhs, _NT, preferred_element_type=F32)
        s = (qx[:, :seq] * p).astype(BF16)
        sv = jnp.dot(s, jnp.concatenate([v, ones_l], axis=1), preferred_element_type=F32)
        nums.append(_per_row(w_rep, dk) * qx[:, seq:seq + dk] + sv[:, :dk])
        den = jnp.where(lane == h, w_rep * qx[:, seq + dk:] + sv[:, dk:], den)
        vtw = (v.T.astype(F32) * wg_t[h:h + 1, :]).astype(BF16)
        upd = jnp.dot(jnp.concatenate([vtw, wg16], axis=0), k, preferred_element_type=F32)
        decay = decay_row[:, h:h + 1]
        c_ref[0, h] = decay * c_mat + upd[:dk]
        n_ref[0, h:h + 1, :] = decay * n_row + upd[dk + h:dk + h + 1, :]
        yield
    recip = 1.0 / jnp.maximum(jnp.abs(den), exp_neg_meff)
    yield
    for h in range(N_HEADS):
        cols = slice(h * dk, (h + 1) * dk)
        hh = nums[h] * _per_row(jnp.broadcast_to(recip[:, h:h + 1], (seq, LANES)), dk)
        ssq_rep = jnp.dot((hh * hh).astype(BF16), ones_v, preferred_element_type=F32)
        hn = hh * _per_row(lax.rsqrt(ssq_rep * (1.0 / dk) + EPS), dk)
        sig_o = proj_ref[rows, 3 * d + h * dk:3 * d + (h + 1) * dk].astype(F32)
        hg_ref[rows, cols] = (hn * gh_ref[:, cols] * sig_o).astype(BF16)
        yield


SCAN_CHUNK_PIECES = 2 * N_HEADS + 2


def _mlstm_long_kernel(x_ref, g_ref, w_ref, wg_ref, bg_ref, gh_ref,
                       hg_ref, c_ref, n_ref, m_ref,
                       proj_a, gate_a, proj_b, gate_b, *, tiles_per_seq):
    i = pl.program_id(0)
    tm = x_ref.shape[0]

    @pl.when(i == 0)
    def _():
        proj_b[...] = jnp.zeros_like(proj_b)
        gate_b[...] = jnp.zeros_like(gate_b)

    @pl.when(lax.rem(jnp.maximum(i - 1, 0), tiles_per_seq) == 0)
    def _():
        c_ref[...] = jnp.zeros_like(c_ref)
        n_ref[...] = jnp.zeros_like(n_ref)
        m_ref[...] = jnp.zeros_like(m_ref)

    n_chunks = tm // MLSTM_CHUNK

    def scan_tile(proj_r, gate_r):
        for c in range(n_chunks):
            yield from _scan_chunk(proj_r, gate_r, gh_ref, hg_ref, c_ref, n_ref, m_ref, c * MLSTM_CHUNK)

    def step(proj_w, gate_w, proj_r, gate_r):
        _zip_work(scan_tile(proj_r, gate_r),
                  _inproj_pieces(x_ref, g_ref, w_ref, wg_ref, bg_ref, proj_w, gate_w),
                  n_chunks * SCAN_CHUNK_PIECES, _n_inproj_pieces(w_ref.shape[0]))

    @pl.when(lax.rem(i, 2) == 0)
    def _():
        step(proj_a, gate_a, proj_b, gate_b)

    @pl.when(lax.rem(i, 2) == 1)
    def _():
        step(proj_b, gate_b, proj_a, gate_a)


def _mlstm_long(x, g, w, wg, bg, g_head, batch, seq, tm):
    t, d = x.shape
    n = w.shape[0]
    dk = d // N_HEADS
    n_tiles = t // tm
    tiles_per_seq = seq // tm
    assert seq % tm == 0 and tm % MLSTM_CHUNK == 0
    prev = lambda i: jnp.maximum(i - 1, 0)
    return pl.pallas_call(
        functools.partial(_mlstm_long_kernel, tiles_per_seq=tiles_per_seq),
        grid=(n_tiles + 1,),
        in_specs=[pl.BlockSpec((tm, d), lambda i: (jnp.minimum(i, n_tiles - 1), 0)),
                  _resident((1, d)), _resident((n, d)), _resident((GATE_COLS, d)),
                  _resident((1, GATE_COLS)), _resident((1, d))],
        out_specs=[pl.BlockSpec((tm, d), lambda i: (prev(i), 0)),
                   pl.BlockSpec((1, N_HEADS, dk, dk), lambda i: (prev(i) // tiles_per_seq, 0, 0, 0)),
                   pl.BlockSpec((1, SUBLANES, dk), lambda i: (prev(i) // tiles_per_seq, 0, 0)),
                   pl.BlockSpec((1, SUBLANES, LANES), lambda i: (prev(i) // tiles_per_seq, 0, 0))],
        out_shape=[jax.ShapeDtypeStruct((t, d), BF16),
                   jax.ShapeDtypeStruct((batch, N_HEADS, dk, dk), F32),
                   jax.ShapeDtypeStruct((batch, SUBLANES, dk), F32),
                   jax.ShapeDtypeStruct((batch, SUBLANES, LANES), F32)],
        scratch_shapes=[pltpu.VMEM((tm, n), BF16), pltpu.VMEM((tm, GATE_COLS), F32),
                        pltpu.VMEM((tm, n), BF16), pltpu.VMEM((tm, GATE_COLS), F32)],
        compiler_params=_params(1),
        name="mlstm_long",
    )(x, g, w, wg, bg, g_head)


def _scan_sample_kernel(q_ref, k_ref, v_ref, so_ref, gate_ref, gh_ref, c_in, n_in, m_in,
                        hg_ref, c_out, n_out, m_out, *, seq):
    bt = c_in.shape[0]
    dk = q_ref.shape[1] // N_HEADS
    step = pl.program_id(0)
    row = lax.broadcasted_iota(jnp.int32, (seq, seq), 0)
    col = lax.broadcasted_iota(jnp.int32, (seq, seq), 1)
    causal = row >= col
    diag = row == col
    lane = lax.broadcasted_iota(jnp.int32, (1, LANES), 1)
    to_row = lambda x_col: jnp.sum(jnp.where(diag, x_col, 0.0), axis=0, keepdims=True)

    qs, ks, vs = q_ref[...].astype(F32), k_ref[...].astype(F32), v_ref[...].astype(F32)
    sig_o = so_ref[...].astype(F32)
    pairs = [(j, h) for j in range(bt) for h in range(N_HEADS)]
    tok = lambda x, j, h: x[j * seq:(j + 1) * seq, h * dk:(h + 1) * dk]

    gate_q = {}
    for j in range(bt):
        gates = gate_ref[j * seq:(j + 1) * seq, :]
        b_col = _cumsum(_log_sigmoid(gates), 0)
        m_prev_row = m_in[pl.ds(step * bt + j, 1), :]
        m_new_row = jnp.zeros((1, LANES), F32)
        for h in range(N_HEADS):
            li_c, b_c = gates[:, h:h + 1], b_col[:, LANES + h:LANES + h + 1]
            m_prev = m_prev_row[:, h:h + 1]
            b_last = b_c[seq - 1:seq, :]
            dm = jnp.where(causal, b_c - to_row(b_c) + to_row(li_c), -jnp.inf)
            inter = b_c + m_prev
            m_eff = jnp.maximum(inter, jnp.max(dm, axis=1, keepdims=True))
            g = b_last - b_c + li_c
            m_new = jnp.maximum(b_last + m_prev, jnp.max(g, axis=0, keepdims=True))
            gate_q[j, h] = dict(w_inter=jnp.exp(inter - m_eff), p=jnp.exp(dm - m_eff),
                                floor=jnp.exp(-m_eff), decay=jnp.exp(b_last + m_prev - m_new),
                                wg=jnp.exp(g - m_new))
            m_new_row = jnp.where(lane == h, m_new, m_new_row)
        m_out[pl.ds(step * bt + j, 1), :] = m_new_row

    qx = {}
    for j, h in pairs:
        n_rows = jnp.concatenate([n_in[j, h:h + 1, :], jnp.zeros((SUBLANES - 1, dk), F32)], axis=0)
        rhs = jnp.concatenate([c_in[j, h], tok(ks, j, h), n_rows], axis=0)
        qx[j, h] = lax.dot_general(tok(qs, j, h), rhs, _NT, preferred_element_type=F32)

    ones = jnp.ones((seq, LANES), F32)
    sv = {}
    for j, h in pairs:
        s = qx[j, h][:, dk:dk + seq] * gate_q[j, h]["p"]
        sv[j, h] = jnp.dot(s, jnp.concatenate([tok(vs, j, h), ones], axis=1),
                           preferred_element_type=F32)

    for j, h in pairs:
        gq = gate_q[j, h]
        num = gq["w_inter"] * qx[j, h][:, :dk] + sv[j, h][:, :dk]
        den = gq["w_inter"] * qx[j, h][:, dk + seq:dk + seq + 1] + sv[j, h][:, dk:dk + 1]
        hh = num * (1.0 / jnp.maximum(jnp.abs(den), gq["floor"]))
        hg_ref[j * seq:(j + 1) * seq, h * dk:(h + 1) * dk] = _gated_head_out(
            hh, gh_ref[:, h * dk:(h + 1) * dk], tok(sig_o, j, h))

    def update_of(j, h):
        return lax.dot_general(tok(vs, j, h) * gate_q[j, h]["wg"], tok(ks, j, h), _TN,
                               preferred_element_type=F32)

    upd_next = update_of(*pairs[0])
    for idx, (j, h) in enumerate(pairs):
        upd = upd_next
        if idx + 1 < len(pairs):
            upd_next = update_of(*pairs[idx + 1])
        gq = gate_q[j, h]
        c_out[j, h] = gq["decay"] * c_in[j, h] + upd
        n_out[j, h:h + 1, :] = (gq["decay"] * n_in[j, h:h + 1, :]
                                + jnp.sum(tok(ks, j, h) * gq["wg"], axis=0, keepdims=True))


def _scan_sample(proj, gates, g_head, c0, n0, m0, seq, bt):
    batch = c0.shape[0]
    d = g_head.shape[1]
    dk = d // N_HEADS
    tok = lambda j: pl.BlockSpec((bt * seq, d), lambda i: (i, j))
    c_spec = pl.BlockSpec((bt, N_HEADS, dk, dk), lambda i: (i, 0, 0, 0))
    n_spec = pl.BlockSpec((bt, N_HEADS, dk), lambda i: (i, 0, 0))
    m_spec = pl.BlockSpec((batch, LANES), lambda i: (0, 0))
    return pl.pallas_call(
        functools.partial(_scan_sample_kernel, seq=seq),
        grid=(batch // bt,),
        in_specs=[tok(0), tok(1), tok(2), tok(3),
                  pl.BlockSpec((bt * seq, GATE_COLS), lambda i: (i, 0)),
                  _resident((1, d)), c_spec, n_spec, m_spec],
        out_specs=[tok(0), c_spec, n_spec, m_spec],
        out_shape=[jax.ShapeDtypeStruct((batch * seq, d), F32),
                   jax.ShapeDtypeStruct(c0.shape, F32),
                   jax.ShapeDtypeStruct(n0.shape, F32),
                   jax.ShapeDtypeStruct((batch, LANES), F32)],
        compiler_params=_params(1),
        name="mlstm_scan_sample",
    )(proj, proj, proj, proj, gates, g_head, c0, n0, m0)


def _post_kernel(x_ref, z_ref, wo_ref, g_ref, wgu_ref, wd_ref, gfin_ref, out_ref,
                 xn_ref, act_ref, *, final_norm):
    x1 = x_ref[...] + jnp.dot(z_ref[...].astype(BF16), wo_ref[...], preferred_element_type=F32)
    out_ref[...] = x1
    xn_ref[...] = _rmsnorm(x1, g_ref[...]).astype(BF16)
    ff = wd_ref.shape[0]
    for c in range(ff // FFN_COLS):
        gate = jnp.dot(xn_ref[...], wgu_ref[:, c * FFN_COLS:(c + 1) * FFN_COLS],
                       preferred_element_type=F32)
        up = jnp.dot(xn_ref[...], wgu_ref[:, ff + c * FFN_COLS:ff + (c + 1) * FFN_COLS],
                     preferred_element_type=F32)
        act_ref[:, c * FFN_COLS:(c + 1) * FFN_COLS] = (gate * _sigmoid(gate) * up).astype(BF16)
    x2 = out_ref[...] + jnp.dot(act_ref[...], wd_ref[...], preferred_element_type=F32)
    if final_norm:
        x2 = _rmsnorm(x2, gfin_ref[...])
    out_ref[...] = x2


def _post(x, z, wo, g, wgu, wd, gfin, layer, tm, final_norm):
    t, d = x.shape
    ff = wd.shape[1]
    assert ff % FFN_COLS == 0
    tile = pl.BlockSpec((tm, d), lambda i: (i, 0))
    of_layer = lambda rows, cols: pl.BlockSpec((None, rows, cols), lambda i: (layer, 0, 0),
                                               pipeline_mode=pl.Buffered(1))
    return pl.pallas_call(
        functools.partial(_post_kernel, final_norm=final_norm),
        grid=(t // tm,),
        in_specs=[tile, tile, _resident((d, d)), _resident((1, d)),
                  of_layer(d, 2 * ff), of_layer(ff, d), _resident((1, d))],
        out_specs=tile,
        out_shape=jax.ShapeDtypeStruct((t, d), F32),
        scratch_shapes=[pltpu.VMEM((tm, d), BF16), pltpu.VMEM((tm, ff), BF16)],
        compiler_params=_params(1),
        name="outproj_ffn_final" if final_norm else "outproj_ffn",
    )(x, z, wo, g, wgu, wd, gfin)


def _conv_taps(u, prev1, prev2, taps_ref):
    return taps_ref[0:1, :] * prev2 + taps_ref[1:2, :] * prev1 + taps_ref[2:3, :] * u


def _conv_proj(x_ref, g_ref, w_ref):
    d = x_ref.shape[1]
    xn = _rmsnorm(x_ref[...], g_ref[...]).astype(BF16)
    part = lambda c: jnp.dot(xn, w_ref[:, c * d:(c + 1) * d], preferred_element_type=F32)
    return part(0), part(1) * part(2)


def _conv_long_kernel(x_ref, g_ref, w_ref, taps_ref, st_ref, z_ref, st_out_ref, carry_ref):
    tm = x_ref.shape[0]

    @pl.when(pl.program_id(1) == 0)
    def _():
        carry_ref[...] = st_ref[0]

    bg, u = _conv_proj(x_ref, g_ref, w_ref)
    row = lax.broadcasted_iota(jnp.int32, (tm, 1), 0)
    c0, c1 = carry_ref[0:1, :], carry_ref[1:2, :]
    prev1 = jnp.where(row == 0, c1, pltpu.roll(u, 1, 0))
    prev2 = jnp.where(row == 0, c0, jnp.where(row == 1, c1, pltpu.roll(u, 2, 0)))
    z_ref[...] = (bg * _conv_taps(u, prev1, prev2, taps_ref)).astype(BF16)
    tail = u[tm - (CONV_W - 1):tm, :]
    carry_ref[...] = tail
    st_out_ref[0] = tail


def _conv_short_kernel(x_ref, g_ref, w_ref, taps_ref, st_ref, z_ref, st_out_ref, *, seq):
    tm, d = x_ref.shape
    nseq = tm // seq
    bg, u = _conv_proj(x_ref, g_ref, w_ref)
    pos = lax.broadcasted_iota(jnp.int32, (1, seq, 1), 1)
    s0, s1 = st_ref[:, 0:1, :], st_ref[:, 1:2, :]
    u3 = u.reshape(nseq, seq, d)
    r1 = pltpu.roll(u, 1, 0).reshape(nseq, seq, d)
    r2 = pltpu.roll(u, 2, 0).reshape(nseq, seq, d)
    prev1 = jnp.where(pos == 0, s1, r1)
    prev2 = jnp.where(pos == 0, s0, jnp.where(pos == 1, s1, r2))
    conv = (taps_ref[0:1, :].reshape(1, 1, d) * prev2 + taps_ref[1:2, :].reshape(1, 1, d) * prev1
            + taps_ref[2:3, :].reshape(1, 1, d) * u3)
    z_ref[...] = (bg * conv.reshape(tm, d)).astype(BF16)
    st_out_ref[...] = u3[:, seq - (CONV_W - 1):seq, :]


def _conv(x, g, w, taps, state, batch, seq, tm):
    t, d = x.shape
    keep = CONV_W - 1
    out_shape = [jax.ShapeDtypeStruct((t, d), BF16), jax.ShapeDtypeStruct((batch, keep, d), F32)]
    weights = [_resident((1, d)), _resident((d, 3 * d)), _resident((SUBLANES, d))]
    if seq >= tm:
        nt = seq // tm
        st_spec = pl.BlockSpec((1, keep, d), lambda b, j: (b, 0, 0))
        tile = pl.BlockSpec((tm, d), lambda b, j: (b * nt + j, 0))
        return pl.pallas_call(
            _conv_long_kernel,
            grid=(batch, nt),
            in_specs=[tile] + weights + [st_spec],
            out_specs=[tile, st_spec],
            out_shape=out_shape,
            scratch_shapes=[pltpu.VMEM((keep, d), F32)],
            compiler_params=_params(2),
            name="gated_conv_long",
        )(x, g, w, taps, state)
    nseq = tm // seq
    st_spec = pl.BlockSpec((nseq, keep, d), lambda i: (i, 0, 0))
    tile = pl.BlockSpec((tm, d), lambda i: (i, 0))
    return pl.pallas_call(
        functools.partial(_conv_short_kernel, seq=seq),
        grid=(t // tm,),
        in_specs=[tile] + weights + [st_spec],
        out_specs=[tile, st_spec],
        out_shape=out_shape,
        compiler_params=_params(1),
        name="gated_conv_short",
    )(x, g, w, taps, state)


def _token_tile(t):
    return min(t, 512)


def _trunk(x, c0, n0, m0, conv0, wts, sample_bt):
    batch, seq, d = x.shape
    t = batch * seq
    tm = _token_tile(t)
    x2d = x.reshape(t, d)
    if c0 is None:
        hg, c_fin, n_pad, m_pad = _mlstm_long(x2d, wts["norm_mix0"], wts["w_in"], wts["w_gate"],
                                              wts["b_gate"], wts["g_head"], batch, seq, tm)
        n_fin, m_fin = n_pad[:, :N_HEADS, :], m_pad[:, 0, :N_HEADS]
    else:
        proj, gates = _inproj(x2d, wts["norm_mix0"], wts["w_in"], wts["w_gate"], wts["b_gate"], tm)
        m0_pad = jnp.pad(m0, ((0, 0), (0, LANES - N_HEADS)))
        hg, c_fin, n_fin, m_pad = _scan_sample(proj, gates, wts["g_head"], c0, n0, m0_pad, seq, sample_bt)
        m_fin = m_pad[:, :N_HEADS]
    x1 = _post(x2d, hg, wts["w_out0"], wts["norm_ffn0"], wts["w_gu"], wts["w_down"],
               wts["norm_final"], 0, tm, final_norm=False)
    z, conv_fin = _conv(x1, wts["norm_mix1"], wts["w_conv_in"], wts["taps"], conv0, batch, seq, tm)
    y = _post(x1, z, wts["w_conv_out"], wts["norm_ffn1"], wts["w_gu"], wts["w_down"],
              wts["norm_final"], 1, tm, final_norm=True)
    return (y.reshape(batch, seq, d), c_fin[None], n_fin[None], m_fin[None], conv_fin[None])


def kernel(x_prompt, x_sample, state_mlstm_C, state_mlstm_n, state_mlstm_m, state_conv, norm_mix, norm_ffn, norm_final, w_mlstm_in, b_mlstm_gate, mlstm_head_norm, w_mlstm_out, w_conv_in, w_conv_taps, w_conv_out, w_ffn_gate_up, w_ffn_down):
    d = x_prompt.shape[-1]
    dk = d // N_HEADS
    assert norm_mix.shape[0] == 2 and w_mlstm_in.shape[0] == 1 and w_conv_in.shape[0] == 1
    assert w_conv_taps.shape[1] == CONV_W and 2 * N_HEADS <= LANES
    row = lambda v: v.reshape(1, -1).astype(F32)
    pad_cols = lambda m: jnp.pad(m, ((0, 0), (0, LANES - N_HEADS)))
    pad_rows = lambda m: jnp.pad(m, ((0, LANES - N_HEADS), (0, 0)))
    w_in_t = jnp.swapaxes(w_mlstm_in[0], 0, 1)
    q_scale = jnp.concatenate([jnp.full((d, 1), dk ** -0.5, F32), jnp.ones((3 * d, 1), F32)])
    wts = {
        "norm_mix0": row(norm_mix[0]), "norm_mix1": row(norm_mix[1]),
        "norm_ffn0": row(norm_ffn[0]), "norm_ffn1": row(norm_ffn[1]),
        "norm_final": row(norm_final), "g_head": row(mlstm_head_norm[0]),
        "w_in": (w_in_t[:4 * d] * q_scale).astype(BF16),
        "w_gate": jnp.concatenate([pad_rows(w_in_t[4 * d:4 * d + N_HEADS]),
                                   pad_rows(w_in_t[4 * d + N_HEADS:])], axis=0).astype(BF16),
        "b_gate": jnp.concatenate([pad_cols(b_mlstm_gate[:, :N_HEADS]),
                                   pad_cols(b_mlstm_gate[:, N_HEADS:])], axis=1).astype(F32),
        "w_out0": w_mlstm_out[0].astype(BF16),
        "w_conv_in": w_conv_in[0].astype(BF16),
        "w_conv_out": w_conv_out[0].astype(BF16),
        "taps": jnp.pad(w_conv_taps[0], ((0, SUBLANES - CONV_W), (0, 0))).astype(F32),
        "w_gu": w_ffn_gate_up.astype(BF16), "w_down": w_ffn_down.astype(BF16),
    }
    bp = x_prompt.shape[0]
    zero_conv = jnp.zeros((bp, CONV_W - 1, d), F32)
    y_p, c_p, n_p, m_p, conv_p = _trunk(x_prompt, None, None, None, zero_conv, wts, None)
    y_s, c_s, n_s, m_s, conv_s = _trunk(
        x_sample, state_mlstm_C[0], state_mlstm_n[0], state_mlstm_m[0], state_conv[0], wts, 4)
    return (y_p, y_s, c_p, n_p, m_p, conv_p, c_s, n_s, m_s, conv_s)
```

```python
import functools

import jax
import jax.numpy as jnp
from jax import lax
from jax.experimental import pallas as pl
from jax.experimental.pallas import tpu as pltpu

EPS = 1e-6
N_HEADS = 4
MLSTM_CHUNK = 128
CONV_W = 3
LANES = 128
SUBLANES = 8
GATE_COLS = 2 * LANES
FFN_COLS = 256
VMEM_LIMIT = 56 * 1024 * 1024

F32 = jnp.float32
BF16 = jnp.bfloat16
_NT = (((1,), (1,)), ((), ()))
_TN = (((0,), (0,)), ((), ()))


def _params(n_grid):
    return pltpu.CompilerParams(
        dimension_semantics=("arbitrary",) * n_grid, vmem_limit_bytes=VMEM_LIMIT)


def _resident(shape):
    zeros = (0,) * len(shape)
    return pl.BlockSpec(shape, lambda *_: zeros, pipeline_mode=pl.Buffered(1))


def _rmsnorm(x, g):
    return x * lax.rsqrt(jnp.mean(x * x, axis=-1, keepdims=True) + EPS) * g


def _sigmoid(x):
    return 1.0 / (1.0 + jnp.exp(-x))


def _log_sigmoid(x):
    return jnp.minimum(x, 0.0) - jnp.log1p(jnp.exp(-jnp.abs(x)))


def _prefix(x, axis, combine, identity):
    n = x.shape[axis]
    idx = lax.broadcasted_iota(jnp.int32, x.shape, axis)
    shift = 1
    while shift < n:
        x = combine(x, jnp.where(idx >= shift, pltpu.roll(x, shift, axis), identity))
        shift *= 2
    return x


def _cumsum(x, axis):
    return _prefix(x, axis, jnp.add, 0.0)


INPROJ_COLS = 256


def _inproj_pieces(x_ref, g_ref, w_ref, wg_ref, bg_ref, proj_ref, gate_ref):
    d = x_ref.shape[1]
    n = w_ref.shape[0]
    xn = _rmsnorm(x_ref[...], g_ref[...]).astype(BF16)
    yield
    for c0 in range(0, n, INPROJ_COLS):
        cols = slice(c0, c0 + INPROJ_COLS)
        y = lax.dot_general(xn, w_ref[cols, :], _NT, preferred_element_type=F32)
        if c0 >= n - d:
            y = _sigmoid(y)
        proj_ref[:, cols] = y.astype(BF16)
        yield
    gate_ref[...] = lax.dot_general(xn, wg_ref[...], _NT, preferred_element_type=F32) + bg_ref[...]
    yield


def _n_inproj_pieces(n):
    return n // INPROJ_COLS + 2


def _inproj_kernel(*refs):
    for _ in _inproj_pieces(*refs):
        pass


def _zip_work(primary, secondary, n_primary, n_secondary):
    next(secondary, None)
    done = 1
    for k, _ in enumerate(primary, start=1):
        while done * n_primary < k * n_secondary:
            next(secondary, None)
            done += 1
    for _ in secondary:
        pass


def _cast_specs(arr, n_steps):
    rows, cols = arr.shape
    pack = 2 * SUBLANES
    nb = max(n for n in range(1, n_steps + 1) if rows % n == 0 and (rows // n) % pack == 0)
    blk = lambda i: (jnp.minimum(i * nb // n_steps, nb - 1), 0)
    spec = pl.BlockSpec((rows // nb, cols), blk)
    return spec, spec, jax.ShapeDtypeStruct((rows, cols), BF16)


def _inproj(x, g, w, wg, bg, tm):
    t, d = x.shape
    n = w.shape[0]
    return pl.pallas_call(
        _inproj_kernel,
        grid=(t // tm,),
        in_specs=[pl.BlockSpec((tm, d), lambda i: (i, 0)),
                  _resident((1, d)), _resident((n, d)), _resident((GATE_COLS, d)), _resident((1, GATE_COLS))],
        out_specs=[pl.BlockSpec((tm, n), lambda i: (i, 0)),
                   pl.BlockSpec((tm, GATE_COLS), lambda i: (i, 0))],
        out_shape=[jax.ShapeDtypeStruct((t, n), BF16), jax.ShapeDtypeStruct((t, GATE_COLS), F32)],
        compiler_params=_params(1),
        name="mlstm_inproj",
    )(x, g, w, wg, bg)


def _gated_head_out(h, g_head, sig_o):
    hn = h * lax.rsqrt(jnp.mean(h * h, axis=-1, keepdims=True) + EPS)
    return hn * g_head * sig_o


def _per_row(rep, width):
    return jnp.concatenate([rep] * (width // LANES), axis=1)


def _scan_chunk(proj_ref, gate_ref, gh_ref, hg_ref, c_ref, n_ref, m_ref, r0):
    seq = MLSTM_CHUNK
    d = gh_ref.shape[1]
    dk = d // N_HEADS
    assert seq == LANES
    rows = slice(r0, r0 + seq)

    lane = lax.broadcasted_iota(jnp.int32, (1, LANES), 1)
    li = gate_ref[rows, 0:LANES]
    b = _prefix(_log_sigmoid(gate_ref[rows, LANES:GATE_COLS]), 0, jnp.add, 0.0)
    a = li - b
    m_prev = m_ref[0, 0:1, :]
    neg_mx = -jnp.maximum(m_prev, _prefix(a, 0, jnp.maximum, -jnp.inf))
    exp_neg_meff = jnp.exp(neg_mx - b)
    b_last = b[seq - 1:seq, :]
    g = b_last - b + li
    m_new = jnp.maximum(b_last + m_prev, jnp.max(g, axis=0, keepdims=True))
    decay_row = jnp.exp(b_last + m_prev - m_new)
    wg_t = jnp.exp(g - m_new).T
    a_t = a.T
    m_ref[0, 0:1, :] = jnp.where(lane < N_HEADS, m_new, 0.0)

    causal = (lax.broadcasted_iota(jnp.int32, (seq, seq), 0)
              >= lax.broadcasted_iota(jnp.int32, (seq, seq), 1))
    ones_l = jnp.ones((seq, LANES), BF16)
    ones_v = jnp.ones((dk, LANES), BF16)
    wg16 = wg_t[0:2 * SUBLANES, :].astype(BF16)
    nums = []
    den = jnp.zeros((seq, LANES), F32)
    yield
    for h in range(N_HEADS):
        q, k, v = (proj_ref[rows, j * d + h * dk:j * d + (h + 1) * dk] for j in range(3))
        c_mat, n_row = c_ref[0, h], n_ref[0, h:h + 1, :]
        neg_mx_rep = jnp.broadcast_to(neg_mx[:, h:h + 1], (seq, LANES))
        p = jnp.exp(jnp.where(causal, neg_mx_rep + a_t[h:h + 1, :], -jnp.inf))
        w_rep = jnp.exp(neg_mx_rep + m_prev[:, h:h + 1])
        rhs = jnp.concatenate(
            [k, c_mat.astype(BF16), jnp.broadcast_to(n_row.astype(BF16), (LANES, dk))], axis=0)
        qx = lax.dot_general(q, rhs, _NT, preferred_element_type=F32)
        s = (qx[:, :seq] * p).astype(BF16)
        sv = jnp.dot(s, jnp.concatenate([v, ones_l], axis=1), preferred_element_type=F32)
        nums.append(_per_row(w_rep, dk) * qx[:, seq:seq + dk] + sv[:, :dk])
        den = jnp.where(lane == h, w_rep * qx[:, seq + dk:] + sv[:, dk:], den)
        vtw = (v.T.astype(F32) * wg_t[h:h + 1, :]).astype(BF16)
        upd = jnp.dot(jnp.concatenate([vtw, wg16], axis=0), k, preferred_element_type=F32)
        decay = decay_row[:, h:h + 1]
        c_ref[0, h] = decay * c_mat + upd[:dk]
        n_ref[0, h:h + 1, :] = decay * n_row + upd[dk + h:dk + h + 1, :]
        yield
    recip = 1.0 / jnp.maximum(jnp.abs(den), exp_neg_meff)
    yield
    for h in range(N_HEADS):
        cols = slice(h * dk, (h + 1) * dk)
        hh = nums[h] * _per_row(jnp.broadcast_to(recip[:, h:h + 1], (seq, LANES)), dk)
        ssq_rep = jnp.dot((hh * hh).astype(BF16), ones_v, preferred_element_type=F32)
        hn = hh * _per_row(lax.rsqrt(ssq_rep * (1.0 / dk) + EPS), dk)
        sig_o = proj_ref[rows, 3 * d + h * dk:3 * d + (h + 1) * dk].astype(F32)
        hg_ref[rows, cols] = (hn * gh_ref[:, cols] * sig_o).astype(BF16)
        yield


SCAN_CHUNK_PIECES = 2 * N_HEADS + 2


def _mlstm_long_kernel(*refs, tiles_per_seq, n_cast):
    x_ref, g_ref, w_ref, wg_ref, bg_ref, gh_ref = refs[:6]
    cast_src = refs[6:6 + n_cast]
    hg_ref, c_ref, n_ref, m_ref = refs[6 + n_cast:10 + n_cast]
    cast_dst = refs[10 + n_cast:10 + 2 * n_cast]
    proj_a, gate_a, proj_b, gate_b = refs[10 + 2 * n_cast:]
    i = pl.program_id(0)
    tm = x_ref.shape[0]
    for src, dst in zip(cast_src, cast_dst):
        dst[...] = src[...].astype(BF16)

    @pl.when(i == 0)
    def _():
        proj_b[...] = jnp.zeros_like(proj_b)
        gate_b[...] = jnp.zeros_like(gate_b)

    @pl.when(lax.rem(jnp.maximum(i - 1, 0), tiles_per_seq) == 0)
    def _():
        c_ref[...] = jnp.zeros_like(c_ref)
        n_ref[...] = jnp.zeros_like(n_ref)
        m_ref[...] = jnp.zeros_like(m_ref)

    n_chunks = tm // MLSTM_CHUNK

    def scan_tile(proj_r, gate_r):
        for c in range(n_chunks):
            yield from _scan_chunk(proj_r, gate_r, gh_ref, hg_ref, c_ref, n_ref, m_ref, c * MLSTM_CHUNK)

    def step(proj_w, gate_w, proj_r, gate_r):
        _zip_work(scan_tile(proj_r, gate_r),
                  _inproj_pieces(x_ref, g_ref, w_ref, wg_ref, bg_ref, proj_w, gate_w),
                  n_chunks * SCAN_CHUNK_PIECES, _n_inproj_pieces(w_ref.shape[0]))

    @pl.when(lax.rem(i, 2) == 0)
    def _():
        step(proj_a, gate_a, proj_b, gate_b)

    @pl.when(lax.rem(i, 2) == 1)
    def _():
        step(proj_b, gate_b, proj_a, gate_a)


def _mlstm_long(x, g, w, wg, bg, g_head, batch, seq, tm, casts):
    t, d = x.shape
    n = w.shape[0]
    dk = d // N_HEADS
    n_tiles = t // tm
    tiles_per_seq = seq // tm
    assert seq % tm == 0 and tm % MLSTM_CHUNK == 0
    prev = lambda i: jnp.maximum(i - 1, 0)
    cast_in, cast_out, cast_shape = zip(*[_cast_specs(a, n_tiles + 1) for a in casts])
    return pl.pallas_call(
        functools.partial(_mlstm_long_kernel, tiles_per_seq=tiles_per_seq, n_cast=len(casts)),
        grid=(n_tiles + 1,),
        in_specs=[pl.BlockSpec((tm, d), lambda i: (jnp.minimum(i, n_tiles - 1), 0)),
                  _resident((1, d)), _resident((n, d)), _resident((GATE_COLS, d)),
                  _resident((1, GATE_COLS)), _resident((1, d)), *cast_in],
        out_specs=[pl.BlockSpec((tm, d), lambda i: (prev(i), 0)),
                   pl.BlockSpec((1, N_HEADS, dk, dk), lambda i: (prev(i) // tiles_per_seq, 0, 0, 0)),
                   pl.BlockSpec((1, SUBLANES, dk), lambda i: (prev(i) // tiles_per_seq, 0, 0)),
                   pl.BlockSpec((1, SUBLANES, LANES), lambda i: (prev(i) // tiles_per_seq, 0, 0)),
                   *cast_out],
        out_shape=[jax.ShapeDtypeStruct((t, d), BF16),
                   jax.ShapeDtypeStruct((batch, N_HEADS, dk, dk), F32),
                   jax.ShapeDtypeStruct((batch, SUBLANES, dk), F32),
                   jax.ShapeDtypeStruct((batch, SUBLANES, LANES), F32), *cast_shape],
        scratch_shapes=[pltpu.VMEM((tm, n), BF16), pltpu.VMEM((tm, GATE_COLS), F32),
                        pltpu.VMEM((tm, n), BF16), pltpu.VMEM((tm, GATE_COLS), F32)],
        compiler_params=_params(1),
        name="mlstm_long",
    )(x, g, w, wg, bg, g_head, *casts)


SCAN_SAMPLE_GROUP = 4


def _n_scan_sample_pieces(bt):
    return 1 + 4 * (bt * N_HEADS // SCAN_SAMPLE_GROUP)


def _scan_sample_pieces(q_ref, k_ref, v_ref, so_ref, gate_ref, gh_ref, c_in, n_in, m_in,
                        hg_ref, c_out, n_out, m_out, *, seq):
    bt = c_in.shape[0]
    dk = q_ref.shape[1] // N_HEADS
    step = pl.program_id(0)
    row = lax.broadcasted_iota(jnp.int32, (seq, seq), 0)
    col = lax.broadcasted_iota(jnp.int32, (seq, seq), 1)
    causal = row >= col
    diag = row == col
    lane = lax.broadcasted_iota(jnp.int32, (1, LANES), 1)
    to_row = lambda x_col: jnp.sum(jnp.where(diag, x_col, 0.0), axis=0, keepdims=True)

    qs, ks, vs = q_ref[...].astype(F32), k_ref[...].astype(F32), v_ref[...].astype(F32)
    sig_o = so_ref[...].astype(F32)
    pairs = [(j, h) for j in range(bt) for h in range(N_HEADS)]
    tok = lambda x, j, h: x[j * seq:(j + 1) * seq, h * dk:(h + 1) * dk]

    gate_q = {}
    for j in range(bt):
        gates = gate_ref[j * seq:(j + 1) * seq, :]
        b_col = _cumsum(_log_sigmoid(gates), 0)
        m_prev_row = m_in[pl.ds(step * bt + j, 1), :]
        m_new_row = jnp.zeros((1, LANES), F32)
        for h in range(N_HEADS):
            li_c, b_c = gates[:, h:h + 1], b_col[:, LANES + h:LANES + h + 1]
            m_prev = m_prev_row[:, h:h + 1]
            b_last = b_c[seq - 1:seq, :]
            dm = jnp.where(causal, b_c - to_row(b_c) + to_row(li_c), -jnp.inf)
            inter = b_c + m_prev
            m_eff = jnp.maximum(inter, jnp.max(dm, axis=1, keepdims=True))
            g = b_last - b_c + li_c
            m_new = jnp.maximum(b_last + m_prev, jnp.max(g, axis=0, keepdims=True))
            gate_q[j, h] = dict(w_inter=jnp.exp(inter - m_eff), p=jnp.exp(dm - m_eff),
                                floor=jnp.exp(-m_eff), decay=jnp.exp(b_last + m_prev - m_new),
                                wg=jnp.exp(g - m_new))
            m_new_row = jnp.where(lane == h, m_new, m_new_row)
        m_out[pl.ds(step * bt + j, 1), :] = m_new_row
    yield

    def grouped(seq_of_pairs):
        for idx, pair in enumerate(seq_of_pairs, start=1):
            yield pair, idx % SCAN_SAMPLE_GROUP == 0

    qx = {}
    for (j, h), pause in grouped(pairs):
        n_rows = jnp.concatenate([n_in[j, h:h + 1, :], jnp.zeros((SUBLANES - 1, dk), F32)], axis=0)
        rhs = jnp.concatenate([c_in[j, h], tok(ks, j, h), n_rows], axis=0)
        qx[j, h] = lax.dot_general(tok(qs, j, h), rhs, _NT, preferred_element_type=F32)
        if pause:
            yield

    ones = jnp.ones((seq, LANES), F32)
    sv = {}
    for (j, h), pause in grouped(pairs):
        s = qx[j, h][:, dk:dk + seq] * gate_q[j, h]["p"]
        sv[j, h] = jnp.dot(s, jnp.concatenate([tok(vs, j, h), ones], axis=1),
                           preferred_element_type=F32)
        if pause:
            yield

    for (j, h), pause in grouped(pairs):
        gq = gate_q[j, h]
        num = gq["w_inter"] * qx[j, h][:, :dk] + sv[j, h][:, :dk]
        den = gq["w_inter"] * qx[j, h][:, dk + seq:dk + seq + 1] + sv[j, h][:, dk:dk + 1]
        hh = num * (1.0 / jnp.maximum(jnp.abs(den), gq["floor"]))
        hg_ref[j * seq:(j + 1) * seq, h * dk:(h + 1) * dk] = _gated_head_out(
            hh, gh_ref[:, h * dk:(h + 1) * dk], tok(sig_o, j, h))
        if pause:
            yield

    def update_of(j, h):
        return lax.dot_general(tok(vs, j, h) * gate_q[j, h]["wg"], tok(ks, j, h), _TN,
                               preferred_element_type=F32)

    upd_next = update_of(*pairs[0])
    for idx, ((j, h), pause) in enumerate(grouped(pairs)):
        upd = upd_next
        if idx + 1 < len(pairs):
            upd_next = update_of(*pairs[idx + 1])
        gq = gate_q[j, h]
        c_out[j, h] = gq["decay"] * c_in[j, h] + upd
        n_out[j, h:h + 1, :] = (gq["decay"] * n_in[j, h:h + 1, :]
                                + jnp.sum(tok(ks, j, h) * gq["wg"], axis=0, keepdims=True))
        if pause:
            yield


def _scan_sample_kernel(*refs, seq):
    for _ in _scan_sample_pieces(*refs, seq=seq):
        pass


def _scan_sample_specs(g_head, c0, n0, seq, bt):
    batch = c0.shape[0]
    d = g_head.shape[1]
    dk = d // N_HEADS
    tok = lambda j: pl.BlockSpec((bt * seq, d), lambda i: (i, j))
    c_spec = pl.BlockSpec((bt, N_HEADS, dk, dk), lambda i: (i, 0, 0, 0))
    n_spec = pl.BlockSpec((bt, N_HEADS, dk), lambda i: (i, 0, 0))
    m_spec = pl.BlockSpec((batch, LANES), lambda i: (0, 0))
    in_specs = [tok(0), tok(1), tok(2), tok(3),
                pl.BlockSpec((bt * seq, GATE_COLS), lambda i: (i, 0)),
                _resident((1, d)), c_spec, n_spec, m_spec]
    out_shapes = [jax.ShapeDtypeStruct((batch * seq, d), F32),
                  jax.ShapeDtypeStruct(c0.shape, F32),
                  jax.ShapeDtypeStruct(n0.shape, F32),
                  jax.ShapeDtypeStruct((batch, LANES), F32)]
    return in_specs, [tok(0), c_spec, n_spec, m_spec], out_shapes


def _scan_sample(proj, gates, g_head, c0, n0, m0, seq, bt):
    in_specs, out_specs, out_shapes = _scan_sample_specs(g_head, c0, n0, seq, bt)
    return pl.pallas_call(
        functools.partial(_scan_sample_kernel, seq=seq),
        grid=(c0.shape[0] // bt,),
        in_specs=in_specs, out_specs=out_specs, out_shape=out_shapes,
        compiler_params=_params(1),
        name="mlstm_scan_sample",
    )(proj, proj, proj, proj, gates, g_head, c0, n0, m0)


def _post_pieces(x_ref, z_ref, wo_ref, g_ref, wgu_ref, wd_ref, gfin_ref, out_ref,
                 xn_ref, act_ref, *, final_norm):
    x1 = x_ref[...] + jnp.dot(z_ref[...].astype(BF16), wo_ref[...], preferred_element_type=F32)
    out_ref[...] = x1
    xn_ref[...] = _rmsnorm(x1, g_ref[...]).astype(BF16)
    yield
    ff = wd_ref.shape[0]
    for c in range(ff // FFN_COLS):
        gate = jnp.dot(xn_ref[...], wgu_ref[:, c * FFN_COLS:(c + 1) * FFN_COLS],
                       preferred_element_type=F32)
        up = jnp.dot(xn_ref[...], wgu_ref[:, ff + c * FFN_COLS:ff + (c + 1) * FFN_COLS],
                     preferred_element_type=F32)
        act_ref[:, c * FFN_COLS:(c + 1) * FFN_COLS] = (gate * _sigmoid(gate) * up).astype(BF16)
        yield
    x2 = out_ref[...] + jnp.dot(act_ref[...], wd_ref[...], preferred_element_type=F32)
    if final_norm:
        x2 = _rmsnorm(x2, gfin_ref[...])
    out_ref[...] = x2
    yield


def _n_post_pieces(ff):
    return ff // FFN_COLS + 2


def _post_kernel(*refs, final_norm, scan_seq):
    if scan_seq is None:
        for _ in _post_pieces(*refs, final_norm=final_norm):
            pass
        return
    post_in, scan_in = refs[:7], refs[7:16]
    out_ref, scan_out = refs[16], refs[17:21]
    xn_ref, act_ref = refs[21:]
    _zip_work(_post_pieces(*post_in, out_ref, xn_ref, act_ref, final_norm=final_norm),
              _scan_sample_pieces(*scan_in, *scan_out, seq=scan_seq),
              _n_post_pieces(post_in[5].shape[0]), _n_scan_sample_pieces(scan_in[6].shape[0]))


def _post(x, z, wo, g, wgu, wd, gfin, layer, tm, final_norm, scan=None):
    t, d = x.shape
    ff = wd.shape[1]
    assert ff % FFN_COLS == 0
    n_steps = t // tm
    tile = pl.BlockSpec((tm, d), lambda i: (i, 0))
    of_layer = lambda rows, cols: pl.BlockSpec((None, rows, cols), lambda i: (layer, 0, 0),
                                               pipeline_mode=pl.Buffered(1))
    in_specs = [tile, tile, _resident((d, d)), _resident((1, d)),
                of_layer(d, 2 * ff), of_layer(ff, d), _resident((1, d))]
    out_specs, out_shapes, operands = [tile], [jax.ShapeDtypeStruct((t, d), F32)], [x, z, wo, g, wgu, wd, gfin]
    scan_seq = None
    if scan is not None:
        proj, gates, g_head, c0, n0, m0, scan_seq = scan
        assert c0.shape[0] % n_steps == 0
        s_in, s_out, s_shapes = _scan_sample_specs(g_head, c0, n0, scan_seq, c0.shape[0] // n_steps)
        in_specs, out_specs, out_shapes = in_specs + s_in, out_specs + s_out, out_shapes + s_shapes
        operands += [proj, proj, proj, proj, gates, g_head, c0, n0, m0]
    return pl.pallas_call(
        functools.partial(_post_kernel, final_norm=final_norm, scan_seq=scan_seq),
        grid=(n_steps,),
        in_specs=in_specs, out_specs=out_specs, out_shape=out_shapes,
        scratch_shapes=[pltpu.VMEM((tm, d), BF16), pltpu.VMEM((tm, ff), BF16)],
        compiler_params=_params(1),
        name="outproj_ffn_final" if final_norm else "outproj_ffn",
    )(*operands)


def _conv_taps(u, prev1, prev2, taps_ref):
    return taps_ref[0:1, :] * prev2 + taps_ref[1:2, :] * prev1 + taps_ref[2:3, :] * u


def _conv_proj(x_ref, g_ref, w_ref):
    d = x_ref.shape[1]
    xn = _rmsnorm(x_ref[...], g_ref[...]).astype(BF16)
    part = lambda c: jnp.dot(xn, w_ref[:, c * d:(c + 1) * d], preferred_element_type=F32)
    return part(0), part(1) * part(2)


def _conv_long_kernel(x_ref, g_ref, w_ref, taps_ref, st_ref, z_ref, st_out_ref, carry_ref):
    tm = x_ref.shape[0]

    @pl.when(pl.program_id(1) == 0)
    def _():
        carry_ref[...] = st_ref[0]

    bg, u = _conv_proj(x_ref, g_ref, w_ref)
    row = lax.broadcasted_iota(jnp.int32, (tm, 1), 0)
    c0, c1 = carry_ref[0:1, :], carry_ref[1:2, :]
    prev1 = jnp.where(row == 0, c1, pltpu.roll(u, 1, 0))
    prev2 = jnp.where(row == 0, c0, jnp.where(row == 1, c1, pltpu.roll(u, 2, 0)))
    z_ref[...] = (bg * _conv_taps(u, prev1, prev2, taps_ref)).astype(BF16)
    tail = u[tm - (CONV_W - 1):tm, :]
    carry_ref[...] = tail
    st_out_ref[0] = tail


def _conv_short_kernel(x_ref, g_ref, w_ref, taps_ref, st_ref, z_ref, st_out_ref, *, seq):
    tm, d = x_ref.shape
    nseq = tm // seq
    bg, u = _conv_proj(x_ref, g_ref, w_ref)
    pos = lax.broadcasted_iota(jnp.int32, (1, seq, 1), 1)
    s0, s1 = st_ref[:, 0:1, :], st_ref[:, 1:2, :]
    u3 = u.reshape(nseq, seq, d)
    r1 = pltpu.roll(u, 1, 0).reshape(nseq, seq, d)
    r2 = pltpu.roll(u, 2, 0).reshape(nseq, seq, d)
    prev1 = jnp.where(pos == 0, s1, r1)
    prev2 = jnp.where(pos == 0, s0, jnp.where(pos == 1, s1, r2))
    conv = (taps_ref[0:1, :].reshape(1, 1, d) * prev2 + taps_ref[1:2, :].reshape(1, 1, d) * prev1
            + taps_ref[2:3, :].reshape(1, 1, d) * u3)
    z_ref[...] = (bg * conv.reshape(tm, d)).astype(BF16)
    st_out_ref[...] = u3[:, seq - (CONV_W - 1):seq, :]


def _conv(x, g, w, taps, state, batch, seq, tm):
    t, d = x.shape
    keep = CONV_W - 1
    out_shape = [jax.ShapeDtypeStruct((t, d), BF16), jax.ShapeDtypeStruct((batch, keep, d), F32)]
    weights = [_resident((1, d)), _resident((d, 3 * d)), _resident((SUBLANES, d))]
    if seq >= tm:
        nt = seq // tm
        st_spec = pl.BlockSpec((1, keep, d), lambda b, j: (b, 0, 0))
        tile = pl.BlockSpec((tm, d), lambda b, j: (b * nt + j, 0))
        return pl.pallas_call(
            _conv_long_kernel,
            grid=(batch, nt),
            in_specs=[tile] + weights + [st_spec],
            out_specs=[tile, st_spec],
            out_shape=out_shape,
            scratch_shapes=[pltpu.VMEM((keep, d), F32)],
            compiler_params=_params(2),
            name="gated_conv_long",
        )(x, g, w, taps, state)
    nseq = tm // seq
    st_spec = pl.BlockSpec((nseq, keep, d), lambda i: (i, 0, 0))
    tile = pl.BlockSpec((tm, d), lambda i: (i, 0))
    return pl.pallas_call(
        functools.partial(_conv_short_kernel, seq=seq),
        grid=(t // tm,),
        in_specs=[tile] + weights + [st_spec],
        out_specs=[tile, st_spec],
        out_shape=out_shape,
        compiler_params=_params(1),
        name="gated_conv_short",
    )(x, g, w, taps, state)


def _token_tile(t):
    return min(t, 512)


def _layer1(x1, conv0, wts, batch, seq, tm):
    z, conv_fin = _conv(x1, wts["norm_mix1"], wts["w_conv_in"], wts["taps"], conv0, batch, seq, tm)
    y, = _post(x1, z, wts["w_conv_out"], wts["norm_ffn1"], wts["w_gu"], wts["w_down"],
               wts["norm_final"], 1, tm, final_norm=True)
    return y.reshape(batch, seq, -1), conv_fin[None]


def kernel(x_prompt, x_sample, state_mlstm_C, state_mlstm_n, state_mlstm_m, state_conv, norm_mix, norm_ffn, norm_final, w_mlstm_in, b_mlstm_gate, mlstm_head_norm, w_mlstm_out, w_conv_in, w_conv_taps, w_conv_out, w_ffn_gate_up, w_ffn_down):
    d = x_prompt.shape[-1]
    dk = d // N_HEADS
    assert norm_mix.shape[0] == 2 and w_mlstm_in.shape[0] == 1 and w_conv_in.shape[0] == 1
    assert w_conv_taps.shape[1] == CONV_W and 2 * N_HEADS <= LANES
    row = lambda v: v.reshape(1, -1).astype(F32)
    pad_cols = lambda m: jnp.pad(m, ((0, 0), (0, LANES - N_HEADS)))
    pad_rows = lambda m: jnp.pad(m, ((0, LANES - N_HEADS), (0, 0)))
    w_in_t = jnp.swapaxes(w_mlstm_in[0], 0, 1)
    q_scale = jnp.concatenate([jnp.full((d, 1), dk ** -0.5, F32), jnp.ones((3 * d, 1), F32)])
    wts = {
        "norm_mix0": row(norm_mix[0]), "norm_mix1": row(norm_mix[1]),
        "norm_ffn0": row(norm_ffn[0]), "norm_ffn1": row(norm_ffn[1]),
        "norm_final": row(norm_final), "g_head": row(mlstm_head_norm[0]),
        "w_in": (w_in_t[:4 * d] * q_scale).astype(BF16),
        "w_gate": jnp.concatenate([pad_rows(w_in_t[4 * d:4 * d + N_HEADS]),
                                   pad_rows(w_in_t[4 * d + N_HEADS:])], axis=0).astype(BF16),
        "b_gate": jnp.concatenate([pad_cols(b_mlstm_gate[:, :N_HEADS]),
                                   pad_cols(b_mlstm_gate[:, N_HEADS:])], axis=1).astype(F32),
        "taps": jnp.pad(w_conv_taps[0], ((0, SUBLANES - CONV_W), (0, 0))).astype(F32),
    }
    bp, sp, _ = x_prompt.shape
    bs, ss, _ = x_sample.shape
    tp, ts = bp * sp, bs * ss
    tm_p, tm_s = _token_tile(tp), _token_tile(ts)
    xp, xs = x_prompt.reshape(tp, d), x_sample.reshape(ts, d)
    n_layers, _, two_ff = w_ffn_gate_up.shape
    ff = two_ff // 2

    hg_p, c_p, n_pad, m_pad, w_gu, w_down, wts["w_conv_in"], wts["w_conv_out"], w_out0 = _mlstm_long(
        xp, wts["norm_mix0"], wts["w_in"], wts["w_gate"], wts["b_gate"], wts["g_head"], bp, sp, tm_p,
        casts=[w_ffn_gate_up.reshape(n_layers * d, two_ff), w_ffn_down.reshape(n_layers * ff, d),
               w_conv_in[0], w_conv_out[0], w_mlstm_out[0]])
    wts["w_gu"], wts["w_down"] = w_gu.reshape(n_layers, d, two_ff), w_down.reshape(n_layers, ff, d)
    n_p, m_p = n_pad[:, :N_HEADS, :], m_pad[:, 0, :N_HEADS]

    proj_s, gates_s = _inproj(xs, wts["norm_mix0"], wts["w_in"], wts["w_gate"], wts["b_gate"], tm_s)
    m0_pad = jnp.pad(state_mlstm_m[0], ((0, 0), (0, LANES - N_HEADS)))
    scan_args = (proj_s, gates_s, wts["g_head"], state_mlstm_C[0], state_mlstm_n[0], m0_pad)
    post0 = functools.partial(_post, wo=w_out0, g=wts["norm_ffn0"], wgu=wts["w_gu"], wd=wts["w_down"],
                              gfin=wts["norm_final"], layer=0, final_norm=False)
    if bs % (tp // tm_p) == 0:
        x1_p, hg_s, c_s, n_s, m_s_pad = post0(xp, hg_p, tm=tm_p, scan=scan_args + (ss,))
    else:
        x1_p, = post0(xp, hg_p, tm=tm_p)
        hg_s, c_s, n_s, m_s_pad = _scan_sample(*scan_args, ss, 4)
    x1_s, = post0(xs, hg_s, tm=tm_s)

    y_p, conv_p = _layer1(x1_p, jnp.zeros((bp, CONV_W - 1, d), F32), wts, bp, sp, tm_p)
    y_s, conv_s = _layer1(x1_s, state_conv[0], wts, bs, ss, tm_s)
    return (y_p, y_s, c_p[None], n_p[None], m_p[None], conv_p,
            c_s[None], n_s[None], m_s_pad[:, :N_HEADS][None], conv_s)
```

```python
import functools

import jax
import jax.numpy as jnp
from jax import lax
from jax.experimental import pallas as pl
from jax.experimental.pallas import tpu as pltpu

EPS = 1e-6
N_HEADS = 4
MLSTM_CHUNK = 128
CONV_W = 3
LANES = 128
SUBLANES = 8
GATE_COLS = 2 * LANES
FFN_COLS = 256
VMEM_LIMIT = 56 * 1024 * 1024

F32 = jnp.float32
BF16 = jnp.bfloat16
_NT = (((1,), (1,)), ((), ()))
_TN = (((0,), (0,)), ((), ()))


def _params(n_grid):
    return pltpu.CompilerParams(
        dimension_semantics=("arbitrary",) * n_grid, vmem_limit_bytes=VMEM_LIMIT)


def _resident(shape):
    zeros = (0,) * len(shape)
    return pl.BlockSpec(shape, lambda *_: zeros, pipeline_mode=pl.Buffered(1))


def _rmsnorm(x, g):
    return x * lax.rsqrt(jnp.mean(x * x, axis=-1, keepdims=True) + EPS) * g


def _sigmoid(x):
    return 1.0 / (1.0 + jnp.exp(-x))


def _log_sigmoid(x):
    return jnp.minimum(x, 0.0) - jnp.log1p(jnp.exp(-jnp.abs(x)))


def _prefix(x, axis, combine, identity):
    n = x.shape[axis]
    idx = lax.broadcasted_iota(jnp.int32, x.shape, axis)
    shift = 1
    while shift < n:
        x = combine(x, jnp.where(idx >= shift, pltpu.roll(x, shift, axis), identity))
        shift *= 2
    return x


def _cumsum(x, axis):
    return _prefix(x, axis, jnp.add, 0.0)


INPROJ_COLS = 256


def _inproj_pieces(x_ref, g_ref, w_ref, wg_ref, bg_ref, proj_ref, gate_ref):
    d = x_ref.shape[1]
    n = w_ref.shape[0]
    xn = _rmsnorm(x_ref[...], g_ref[...]).astype(BF16)
    yield
    for c0 in range(0, n, INPROJ_COLS):
        cols = slice(c0, c0 + INPROJ_COLS)
        y = lax.dot_general(xn, w_ref[cols, :], _NT, preferred_element_type=F32)
        if c0 >= n - d:
            y = _sigmoid(y)
        proj_ref[:, cols] = y.astype(BF16)
        yield
    gate_ref[...] = lax.dot_general(xn, wg_ref[...], _NT, preferred_element_type=F32) + bg_ref[...]
    yield


def _n_inproj_pieces(n):
    return n // INPROJ_COLS + 2


def _inproj_kernel(*refs):
    for _ in _inproj_pieces(*refs):
        pass


def _zip_work(primary, secondary, n_primary, n_secondary):
    next(secondary, None)
    done = 1
    for k, _ in enumerate(primary, start=1):
        while done * n_primary < k * n_secondary:
            next(secondary, None)
            done += 1
    for _ in secondary:
        pass


def _cast_specs(arr, n_steps):
    rows, cols = arr.shape
    pack = 2 * SUBLANES
    nb = max(n for n in range(1, n_steps + 1) if rows % n == 0 and (rows // n) % pack == 0)
    blk = lambda i: (jnp.minimum(i * nb // n_steps, nb - 1), 0)
    spec = pl.BlockSpec((rows // nb, cols), blk)
    return spec, spec, jax.ShapeDtypeStruct((rows, cols), BF16)


def _inproj(x, g, w, wg, bg, tm):
    t, d = x.shape
    n = w.shape[0]
    return pl.pallas_call(
        _inproj_kernel,
        grid=(t // tm,),
        in_specs=[pl.BlockSpec((tm, d), lambda i: (i, 0)),
                  _resident((1, d)), _resident((n, d)), _resident((GATE_COLS, d)), _resident((1, GATE_COLS))],
        out_specs=[pl.BlockSpec((tm, n), lambda i: (i, 0)),
                   pl.BlockSpec((tm, GATE_COLS), lambda i: (i, 0))],
        out_shape=[jax.ShapeDtypeStruct((t, n), BF16), jax.ShapeDtypeStruct((t, GATE_COLS), F32)],
        compiler_params=_params(1),
        name="mlstm_inproj",
    )(x, g, w, wg, bg)


def _gated_head_out(h, g_head, sig_o):
    hn = h * lax.rsqrt(jnp.mean(h * h, axis=-1, keepdims=True) + EPS)
    return hn * g_head * sig_o


def _per_row(rep, width):
    return jnp.concatenate([rep] * (width // LANES), axis=1)


def _scan_chunk(proj_ref, gate_ref, gh_ref, hg_ref, c_ref, n_ref, m_ref, r0):
    seq = MLSTM_CHUNK
    d = gh_ref.shape[1]
    dk = d // N_HEADS
    assert seq == LANES
    rows = slice(r0, r0 + seq)

    lane = lax.broadcasted_iota(jnp.int32, (1, LANES), 1)
    li = gate_ref[rows, 0:LANES]
    b = _prefix(_log_sigmoid(gate_ref[rows, LANES:GATE_COLS]), 0, jnp.add, 0.0)
    a = li - b
    m_prev = m_ref[0, 0:1, :]
    neg_mx = -jnp.maximum(m_prev, _prefix(a, 0, jnp.maximum, -jnp.inf))
    exp_neg_meff = jnp.exp(neg_mx - b)
    b_last = b[seq - 1:seq, :]
    g = b_last - b + li
    m_new = jnp.maximum(b_last + m_prev, jnp.max(g, axis=0, keepdims=True))
    decay_row = jnp.exp(b_last + m_prev - m_new)
    wg_t = jnp.exp(g - m_new).T
    a_t = a.T
    m_ref[0, 0:1, :] = jnp.where(lane < N_HEADS, m_new, 0.0)

    causal = (lax.broadcasted_iota(jnp.int32, (seq, seq), 0)
              >= lax.broadcasted_iota(jnp.int32, (seq, seq), 1))
    ones_l = jnp.ones((seq, LANES), BF16)
    ones_v = jnp.ones((dk, LANES), BF16)
    wg16 = wg_t[0:2 * SUBLANES, :].astype(BF16)
    den = jnp.zeros((seq, LANES), F32)
    heads = range(N_HEADS)
    tok = lambda j, h: proj_ref[rows, j * d + h * dk:j * d + (h + 1) * dk]
    yield
    p, w_rep, qx, upd = [], [], [], []
    for h in heads:
        neg_mx_rep = jnp.broadcast_to(neg_mx[:, h:h + 1], (seq, LANES))
        p.append(jnp.exp(jnp.where(causal, neg_mx_rep + a_t[h:h + 1, :], -jnp.inf)))
        w_rep.append(jnp.exp(neg_mx_rep + m_prev[:, h:h + 1]))
        rhs = jnp.concatenate([tok(1, h), c_ref[0, h].astype(BF16),
                               jnp.broadcast_to(n_ref[0, h:h + 1, :].astype(BF16), (LANES, dk))], axis=0)
        qx.append(lax.dot_general(tok(0, h), rhs, _NT, preferred_element_type=F32))
        vtw = (tok(2, h).T.astype(F32) * wg_t[h:h + 1, :]).astype(BF16)
        upd.append(jnp.dot(jnp.concatenate([vtw, wg16], axis=0), tok(1, h), preferred_element_type=F32))
        yield
    sv = []
    for h in heads:
        s = (qx[h][:, :seq] * p[h]).astype(BF16)
        sv.append(jnp.dot(s, jnp.concatenate([tok(2, h), ones_l], axis=1), preferred_element_type=F32))
        yield
    nums = []
    for h in heads:
        nums.append(_per_row(w_rep[h], dk) * qx[h][:, seq:seq + dk] + sv[h][:, :dk])
        den = jnp.where(lane == h, w_rep[h] * qx[h][:, seq + dk:] + sv[h][:, dk:], den)
        decay = decay_row[:, h:h + 1]
        c_ref[0, h] = decay * c_ref[0, h] + upd[h][:dk]
        n_ref[0, h:h + 1, :] = decay * n_ref[0, h:h + 1, :] + upd[h][dk + h:dk + h + 1, :]
        yield
    recip = 1.0 / jnp.maximum(jnp.abs(den), exp_neg_meff)
    yield
    hh, ssq_rep = [], []
    for h in heads:
        hh.append(nums[h] * _per_row(jnp.broadcast_to(recip[:, h:h + 1], (seq, LANES)), dk))
        ssq_rep.append(jnp.dot((hh[h] * hh[h]).astype(BF16), ones_v, preferred_element_type=F32))
        yield
    for h in heads:
        cols = slice(h * dk, (h + 1) * dk)
        hn = hh[h] * _per_row(lax.rsqrt(ssq_rep[h] * (1.0 / dk) + EPS), dk)
        hg_ref[rows, cols] = (hn * gh_ref[:, cols] * tok(3, h).astype(F32)).astype(BF16)
        yield


SCAN_CHUNK_PIECES = 5 * N_HEADS + 2


def _mlstm_long_kernel(*refs, tiles_per_seq, n_cast):
    x_ref, g_ref, w_ref, wg_ref, bg_ref, gh_ref = refs[:6]
    cast_src = refs[6:6 + n_cast]
    hg_ref, c_ref, n_ref, m_ref = refs[6 + n_cast:10 + n_cast]
    cast_dst = refs[10 + n_cast:10 + 2 * n_cast]
    proj_a, gate_a, proj_b, gate_b = refs[10 + 2 * n_cast:]
    i = pl.program_id(0)
    tm = x_ref.shape[0]

    @pl.when(i == 0)
    def _():
        proj_b[...] = jnp.zeros_like(proj_b)
        gate_b[...] = jnp.zeros_like(gate_b)

    @pl.when(lax.rem(jnp.maximum(i - 1, 0), tiles_per_seq) == 0)
    def _():
        c_ref[...] = jnp.zeros_like(c_ref)
        n_ref[...] = jnp.zeros_like(n_ref)
        m_ref[...] = jnp.zeros_like(m_ref)

    n_chunks = tm // MLSTM_CHUNK

    def scan_tile(proj_r, gate_r):
        for c in range(n_chunks):
            yield from _scan_chunk(proj_r, gate_r, gh_ref, hg_ref, c_ref, n_ref, m_ref, c * MLSTM_CHUNK)

    def project_and_cast(proj_w, gate_w):
        casts = list(zip(cast_src, cast_dst))
        for k, _ in enumerate(_inproj_pieces(x_ref, g_ref, w_ref, wg_ref, bg_ref, proj_w, gate_w)):
            if casts and k % 3 == 1:
                src, dst = casts.pop()
                dst[...] = src[...].astype(BF16)
            yield
        for src, dst in casts:
            dst[...] = src[...].astype(BF16)

    def step(proj_w, gate_w, proj_r, gate_r):
        _zip_work(scan_tile(proj_r, gate_r), project_and_cast(proj_w, gate_w),
                  n_chunks * SCAN_CHUNK_PIECES, _n_inproj_pieces(w_ref.shape[0]))

    @pl.when(lax.rem(i, 2) == 0)
    def _():
        step(proj_a, gate_a, proj_b, gate_b)

    @pl.when(lax.rem(i, 2) == 1)
    def _():
        step(proj_b, gate_b, proj_a, gate_a)


def _mlstm_long(x, g, w, wg, bg, g_head, batch, seq, tm, casts):
    t, d = x.shape
    n = w.shape[0]
    dk = d // N_HEADS
    n_tiles = t // tm
    tiles_per_seq = seq // tm
    assert seq % tm == 0 and tm % MLSTM_CHUNK == 0
    prev = lambda i: jnp.maximum(i - 1, 0)
    cast_in, cast_out, cast_shape = zip(*[_cast_specs(a, n_tiles + 1) for a in casts])
    return pl.pallas_call(
        functools.partial(_mlstm_long_kernel, tiles_per_seq=tiles_per_seq, n_cast=len(casts)),
        grid=(n_tiles + 1,),
        in_specs=[pl.BlockSpec((tm, d), lambda i: (jnp.minimum(i, n_tiles - 1), 0)),
                  _resident((1, d)), _resident((n, d)), _resident((GATE_COLS, d)),
                  _resident((1, GATE_COLS)), _resident((1, d)), *cast_in],
        out_specs=[pl.BlockSpec((tm, d), lambda i: (prev(i), 0)),
                   pl.BlockSpec((1, N_HEADS, dk, dk), lambda i: (prev(i) // tiles_per_seq, 0, 0, 0)),
                   pl.BlockSpec((1, SUBLANES, dk), lambda i: (prev(i) // tiles_per_seq, 0, 0)),
                   pl.BlockSpec((1, SUBLANES, LANES), lambda i: (prev(i) // tiles_per_seq, 0, 0)),
                   *cast_out],
        out_shape=[jax.ShapeDtypeStruct((t, d), BF16),
                   jax.ShapeDtypeStruct((batch, N_HEADS, dk, dk), F32),
                   jax.ShapeDtypeStruct((batch, SUBLANES, dk), F32),
                   jax.ShapeDtypeStruct((batch, SUBLANES, LANES), F32), *cast_shape],
        scratch_shapes=[pltpu.VMEM((tm, n), BF16), pltpu.VMEM((tm, GATE_COLS), F32),
                        pltpu.VMEM((tm, n), BF16), pltpu.VMEM((tm, GATE_COLS), F32)],
        compiler_params=_params(1),
        name="mlstm_long",
    )(x, g, w, wg, bg, g_head, *casts)


SCAN_SAMPLE_GROUP = 4


def _n_scan_sample_pieces(bt):
    return 1 + 4 * (bt * N_HEADS // SCAN_SAMPLE_GROUP)


def _scan_sample_pieces(q_ref, k_ref, v_ref, so_ref, gate_ref, gh_ref, c_in, n_in, m_in,
                        hg_ref, c_out, n_out, m_out, *, seq):
    bt = c_in.shape[0]
    dk = q_ref.shape[1] // N_HEADS
    step = pl.program_id(0)
    row = lax.broadcasted_iota(jnp.int32, (seq, seq), 0)
    col = lax.broadcasted_iota(jnp.int32, (seq, seq), 1)
    causal = row >= col
    diag = row == col
    lane = lax.broadcasted_iota(jnp.int32, (1, LANES), 1)
    to_row = lambda x_col: jnp.sum(jnp.where(diag, x_col, 0.0), axis=0, keepdims=True)

    qs, ks, vs = q_ref[...].astype(F32), k_ref[...].astype(F32), v_ref[...].astype(F32)
    sig_o = so_ref[...].astype(F32)
    pairs = [(j, h) for j in range(bt) for h in range(N_HEADS)]
    tok = lambda x, j, h: x[j * seq:(j + 1) * seq, h * dk:(h + 1) * dk]

    gate_q = {}
    for j in range(bt):
        gates = gate_ref[j * seq:(j + 1) * seq, :]
        b_col = _cumsum(_log_sigmoid(gates), 0)
        m_prev_row = m_in[pl.ds(step * bt + j, 1), :]
        m_new_row = jnp.zeros((1, LANES), F32)
        for h in range(N_HEADS):
            li_c, b_c = gates[:, h:h + 1], b_col[:, LANES + h:LANES + h + 1]
            m_prev = m_prev_row[:, h:h + 1]
            b_last = b_c[seq - 1:seq, :]
            dm = jnp.where(causal, b_c - to_row(b_c) + to_row(li_c), -jnp.inf)
            inter = b_c + m_prev
            m_eff = jnp.maximum(inter, jnp.max(dm, axis=1, keepdims=True))
            g = b_last - b_c + li_c
            m_new = jnp.maximum(b_last + m_prev, jnp.max(g, axis=0, keepdims=True))
            gate_q[j, h] = dict(w_inter=jnp.exp(inter - m_eff), p=jnp.exp(dm - m_eff),
                                floor=jnp.exp(-m_eff), decay=jnp.exp(b_last + m_prev - m_new),
                                wg=jnp.exp(g - m_new))
            m_new_row = jnp.where(lane == h, m_new, m_new_row)
        m_out[pl.ds(step * bt + j, 1), :] = m_new_row
    yield

    def grouped(seq_of_pairs):
        for idx, pair in enumerate(seq_of_pairs, start=1):
            yield pair, idx % SCAN_SAMPLE_GROUP == 0

    qx = {}
    for (j, h), pause in grouped(pairs):
        n_rows = jnp.concatenate([n_in[j, h:h + 1, :], jnp.zeros((SUBLANES - 1, dk), F32)], axis=0)
        rhs = jnp.concatenate([c_in[j, h], tok(ks, j, h), n_rows], axis=0)
        qx[j, h] = lax.dot_general(tok(qs, j, h), rhs, _NT, preferred_element_type=F32)
        if pause:
            yield

    ones = jnp.ones((seq, LANES), F32)
    sv = {}
    for (j, h), pause in grouped(pairs):
        s = qx[j, h][:, dk:dk + seq] * gate_q[j, h]["p"]
        sv[j, h] = jnp.dot(s, jnp.concatenate([tok(vs, j, h), ones], axis=1),
                           preferred_element_type=F32)
        if pause:
            yield

    for (j, h), pause in grouped(pairs):
        gq = gate_q[j, h]
        num = gq["w_inter"] * qx[j, h][:, :dk] + sv[j, h][:, :dk]
        den = gq["w_inter"] * qx[j, h][:, dk + seq:dk + seq + 1] + sv[j, h][:, dk:dk + 1]
        hh = num * (1.0 / jnp.maximum(jnp.abs(den), gq["floor"]))
        hg_ref[j * seq:(j + 1) * seq, h * dk:(h + 1) * dk] = _gated_head_out(
            hh, gh_ref[:, h * dk:(h + 1) * dk], tok(sig_o, j, h))
        if pause:
            yield

    def update_of(j, h):
        return lax.dot_general(tok(vs, j, h) * gate_q[j, h]["wg"], tok(ks, j, h), _TN,
                               preferred_element_type=F32)

    upd_next = update_of(*pairs[0])
    for idx, ((j, h), pause) in enumerate(grouped(pairs)):
        upd = upd_next
        if idx + 1 < len(pairs):
            upd_next = update_of(*pairs[idx + 1])
        gq = gate_q[j, h]
        c_out[j, h] = gq["decay"] * c_in[j, h] + upd
        n_out[j, h:h + 1, :] = (gq["decay"] * n_in[j, h:h + 1, :]
                                + jnp.sum(tok(ks, j, h) * gq["wg"], axis=0, keepdims=True))
        if pause:
            yield


def _scan_sample_kernel(*refs, seq):
    for _ in _scan_sample_pieces(*refs, seq=seq):
        pass


def _scan_sample_specs(g_head, c0, n0, seq, bt):
    batch = c0.shape[0]
    d = g_head.shape[1]
    dk = d // N_HEADS
    tok = lambda j: pl.BlockSpec((bt * seq, d), lambda i: (i, j))
    c_spec = pl.BlockSpec((bt, N_HEADS, dk, dk), lambda i: (i, 0, 0, 0))
    n_spec = pl.BlockSpec((bt, N_HEADS, dk), lambda i: (i, 0, 0))
    m_spec = pl.BlockSpec((batch, LANES), lambda i: (0, 0))
    in_specs = [tok(0), tok(1), tok(2), tok(3),
                pl.BlockSpec((bt * seq, GATE_COLS), lambda i: (i, 0)),
                _resident((1, d)), c_spec, n_spec, m_spec]
    out_shapes = [jax.ShapeDtypeStruct((batch * seq, d), F32),
                  jax.ShapeDtypeStruct(c0.shape, F32),
                  jax.ShapeDtypeStruct(n0.shape, F32),
                  jax.ShapeDtypeStruct((batch, LANES), F32)]
    return in_specs, [tok(0), c_spec, n_spec, m_spec], out_shapes


def _scan_sample(proj, gates, g_head, c0, n0, m0, seq, bt):
    in_specs, out_specs, out_shapes = _scan_sample_specs(g_head, c0, n0, seq, bt)
    return pl.pallas_call(
        functools.partial(_scan_sample_kernel, seq=seq),
        grid=(c0.shape[0] // bt,),
        in_specs=in_specs, out_specs=out_specs, out_shape=out_shapes,
        compiler_params=_params(1),
        name="mlstm_scan_sample",
    )(proj, proj, proj, proj, gates, g_head, c0, n0, m0)


def _post_pieces(x_ref, z_ref, wo_ref, g_ref, wgu_ref, wd_ref, gfin_ref, out_ref,
                 xn_ref, act_ref, *, final_norm):
    x1 = x_ref[...] + jnp.dot(z_ref[...].astype(BF16), wo_ref[...], preferred_element_type=F32)
    out_ref[...] = x1
    xn_ref[...] = _rmsnorm(x1, g_ref[...]).astype(BF16)
    yield
    ff = wd_ref.shape[0]
    for c in range(ff // FFN_COLS):
        gate = jnp.dot(xn_ref[...], wgu_ref[:, c * FFN_COLS:(c + 1) * FFN_COLS],
                       preferred_element_type=F32)
        up = jnp.dot(xn_ref[...], wgu_ref[:, ff + c * FFN_COLS:ff + (c + 1) * FFN_COLS],
                     preferred_element_type=F32)
        act_ref[:, c * FFN_COLS:(c + 1) * FFN_COLS] = (gate * _sigmoid(gate) * up).astype(BF16)
        yield
    x2 = out_ref[...] + jnp.dot(act_ref[...], wd_ref[...], preferred_element_type=F32)
    if final_norm:
        x2 = _rmsnorm(x2, gfin_ref[...])
    out_ref[...] = x2
    yield


def _n_post_pieces(ff):
    return ff // FFN_COLS + 2


def _post_kernel(*refs, final_norm, scan_seq):
    if scan_seq is None:
        for _ in _post_pieces(*refs, final_norm=final_norm):
            pass
        return
    post_in, scan_in = refs[:7], refs[7:16]
    out_ref, scan_out = refs[16], refs[17:21]
    xn_ref, act_ref = refs[21:]
    _zip_work(_post_pieces(*post_in, out_ref, xn_ref, act_ref, final_norm=final_norm),
              _scan_sample_pieces(*scan_in, *scan_out, seq=scan_seq),
              _n_post_pieces(post_in[5].shape[0]), _n_scan_sample_pieces(scan_in[6].shape[0]))


def _post(x, z, wo, g, wgu, wd, gfin, layer, tm, final_norm, scan=None):
    t, d = x.shape
    ff = wd.shape[1]
    assert ff % FFN_COLS == 0
    n_steps = t // tm
    tile = pl.BlockSpec((tm, d), lambda i: (i, 0))
    of_layer = lambda rows, cols: pl.BlockSpec((None, rows, cols), lambda i: (layer, 0, 0),
                                               pipeline_mode=pl.Buffered(1))
    in_specs = [tile, tile, _resident((d, d)), _resident((1, d)),
                of_layer(d, 2 * ff), of_layer(ff, d), _resident((1, d))]
    out_specs, out_shapes, operands = [tile], [jax.ShapeDtypeStruct((t, d), F32)], [x, z, wo, g, wgu, wd, gfin]
    scan_seq = None
    if scan is not None:
        proj, gates, g_head, c0, n0, m0, scan_seq = scan
        assert c0.shape[0] % n_steps == 0
        s_in, s_out, s_shapes = _scan_sample_specs(g_head, c0, n0, scan_seq, c0.shape[0] // n_steps)
        in_specs, out_specs, out_shapes = in_specs + s_in, out_specs + s_out, out_shapes + s_shapes
        operands += [proj, proj, proj, proj, gates, g_head, c0, n0, m0]
    return pl.pallas_call(
        functools.partial(_post_kernel, final_norm=final_norm, scan_seq=scan_seq),
        grid=(n_steps,),
        in_specs=in_specs, out_specs=out_specs, out_shape=out_shapes,
        scratch_shapes=[pltpu.VMEM((tm, d), BF16), pltpu.VMEM((tm, ff), BF16)],
        compiler_params=_params(1),
        name="outproj_ffn_final" if final_norm else "outproj_ffn",
    )(*operands)


def _conv_proj(x_ref, g_ref, w_ref):
    d = x_ref.shape[1]
    xn = _rmsnorm(x_ref[...], g_ref[...]).astype(BF16)
    part = lambda c: jnp.dot(xn, w_ref[:, c * d:(c + 1) * d], preferred_element_type=F32)
    return part(0), part(1) * part(2)


def _conv_long_pieces(x_ref, g_ref, w_ref, taps_ref, st_out_ref, carry_ref, z_ref, x_copy_ref):
    tm, d = x_ref.shape
    x = x_ref[...]
    x_copy_ref[...] = x
    xn = _rmsnorm(x, g_ref[...]).astype(BF16)
    row = lax.broadcasted_iota(jnp.int32, (tm, 1), 0)
    yield
    for c0 in range(0, d, INPROJ_COLS):
        cols = slice(c0, c0 + INPROJ_COLS)
        part = lambda k: jnp.dot(xn, w_ref[:, k * d + c0:k * d + c0 + INPROJ_COLS],
                                 preferred_element_type=F32)
        bg, u = part(0), part(1) * part(2)
        yield
        s0, s1 = carry_ref[0:1, cols], carry_ref[1:2, cols]
        prev1 = jnp.where(row == 0, s1, pltpu.roll(u, 1, 0))
        prev2 = jnp.where(row == 0, s0, jnp.where(row == 1, s1, pltpu.roll(u, 2, 0)))
        conv = taps_ref[0:1, cols] * prev2 + taps_ref[1:2, cols] * prev1 + taps_ref[2:3, cols] * u
        z_ref[:, cols] = (bg * conv).astype(BF16)
        tail = u[tm - (CONV_W - 1):tm, :]
        carry_ref[:, cols] = tail
        st_out_ref[0, :, cols] = tail
        yield


def _n_conv_long_pieces(d):
    return 1 + 2 * (d // INPROJ_COLS)


def _layer1_long_kernel(x_ref, gmix_ref, wcin_ref, taps_ref, st_ref, wo_ref, gffn_ref, wgu_ref,
                        wd_ref, gfin_ref, out_ref, st_out_ref,
                        carry_ref, z_a, x_a, z_b, x_b, xn_ref, act_ref, *, tiles_per_seq, n_tiles):
    i = pl.program_id(0)

    @pl.when(i == 0)
    def _():
        z_b[...] = jnp.zeros_like(z_b)
        x_b[...] = jnp.zeros_like(x_b)

    @pl.when(jnp.logical_and(i < n_tiles, lax.rem(i, tiles_per_seq) == 0))
    def _():
        carry_ref[...] = st_ref[0]

    def step(z_w, x_w, z_r, x_r):
        _zip_work(_post_pieces(x_r, z_r, wo_ref, gffn_ref, wgu_ref, wd_ref, gfin_ref, out_ref,
                               xn_ref, act_ref, final_norm=True),
                  _conv_long_pieces(x_ref, gmix_ref, wcin_ref, taps_ref, st_out_ref, carry_ref, z_w, x_w),
                  _n_post_pieces(wd_ref.shape[0]), _n_conv_long_pieces(x_ref.shape[1]))

    @pl.when(lax.rem(i, 2) == 0)
    def _():
        step(z_a, x_a, z_b, x_b)

    @pl.when(lax.rem(i, 2) == 1)
    def _():
        step(z_b, x_b, z_a, x_a)


def _layer1_long(x, gmix, wcin, taps, state, wo, gffn, wgu, wd, gfin, layer, batch, seq, tm):
    t, d = x.shape
    ff = wd.shape[1]
    keep = CONV_W - 1
    n_tiles = t // tm
    tiles_per_seq = seq // tm
    assert seq % tm == 0
    cur = lambda i: jnp.minimum(i, n_tiles - 1)
    st_spec = pl.BlockSpec((1, keep, d), lambda i: (cur(i) // tiles_per_seq, 0, 0))
    of_layer = lambda rows, cols: pl.BlockSpec((None, rows, cols), lambda i: (layer, 0, 0),
                                               pipeline_mode=pl.Buffered(1))
    return pl.pallas_call(
        functools.partial(_layer1_long_kernel, tiles_per_seq=tiles_per_seq, n_tiles=n_tiles),
        grid=(n_tiles + 1,),
        in_specs=[pl.BlockSpec((tm, d), lambda i: (cur(i), 0)),
                  _resident((1, d)), _resident((d, 3 * d)), _resident((SUBLANES, d)), st_spec,
                  _resident((d, d)), _resident((1, d)), of_layer(d, 2 * ff), of_layer(ff, d),
                  _resident((1, d))],
        out_specs=[pl.BlockSpec((tm, d), lambda i: (jnp.maximum(i - 1, 0), 0)), st_spec],
        out_shape=[jax.ShapeDtypeStruct((t, d), F32), jax.ShapeDtypeStruct((batch, keep, d), F32)],
        scratch_shapes=[pltpu.VMEM((keep, d), F32),
                        pltpu.VMEM((tm, d), BF16), pltpu.VMEM((tm, d), F32),
                        pltpu.VMEM((tm, d), BF16), pltpu.VMEM((tm, d), F32),
                        pltpu.VMEM((tm, d), BF16), pltpu.VMEM((tm, ff), BF16)],
        compiler_params=_params(1),
        name="layer1_long",
    )(x, gmix, wcin, taps, state, wo, gffn, wgu, wd, gfin)


def _conv_short_kernel(x_ref, g_ref, w_ref, taps_ref, st_ref, z_ref, st_out_ref, *, seq):
    tm, d = x_ref.shape
    nseq = tm // seq
    bg, u = _conv_proj(x_ref, g_ref, w_ref)
    pos = lax.broadcasted_iota(jnp.int32, (1, seq, 1), 1)
    s0, s1 = st_ref[:, 0:1, :], st_ref[:, 1:2, :]
    u3 = u.reshape(nseq, seq, d)
    r1 = pltpu.roll(u, 1, 0).reshape(nseq, seq, d)
    r2 = pltpu.roll(u, 2, 0).reshape(nseq, seq, d)
    prev1 = jnp.where(pos == 0, s1, r1)
    prev2 = jnp.where(pos == 0, s0, jnp.where(pos == 1, s1, r2))
    conv = (taps_ref[0:1, :].reshape(1, 1, d) * prev2 + taps_ref[1:2, :].reshape(1, 1, d) * prev1
            + taps_ref[2:3, :].reshape(1, 1, d) * u3)
    z_ref[...] = (bg * conv.reshape(tm, d)).astype(BF16)
    st_out_ref[...] = u3[:, seq - (CONV_W - 1):seq, :]


def _conv_short(x, g, w, taps, state, batch, seq, tm):
    t, d = x.shape
    keep = CONV_W - 1
    assert tm % seq == 0
    out_shape = [jax.ShapeDtypeStruct((t, d), BF16), jax.ShapeDtypeStruct((batch, keep, d), F32)]
    weights = [_resident((1, d)), _resident((d, 3 * d)), _resident((SUBLANES, d))]
    nseq = tm // seq
    st_spec = pl.BlockSpec((nseq, keep, d), lambda i: (i, 0, 0))
    tile = pl.BlockSpec((tm, d), lambda i: (i, 0))
    return pl.pallas_call(
        functools.partial(_conv_short_kernel, seq=seq),
        grid=(t // tm,),
        in_specs=[tile] + weights + [st_spec],
        out_specs=[tile, st_spec],
        out_shape=out_shape,
        compiler_params=_params(1),
        name="gated_conv_short",
    )(x, g, w, taps, state)


def _token_tile(t):
    return min(t, 512)


def _layer1(x1, conv0, wts, batch, seq, tm):
    if seq >= tm:
        y, conv_fin = _layer1_long(x1, wts["norm_mix1"], wts["w_conv_in"], wts["taps"], conv0,
                                   wts["w_conv_out"], wts["norm_ffn1"], wts["w_gu"], wts["w_down"],
                                   wts["norm_final"], 1, batch, seq, tm)
    else:
        z, conv_fin = _conv_short(x1, wts["norm_mix1"], wts["w_conv_in"], wts["taps"], conv0,
                                  batch, seq, tm)
        y, = _post(x1, z, wts["w_conv_out"], wts["norm_ffn1"], wts["w_gu"], wts["w_down"],
                   wts["norm_final"], 1, tm, final_norm=True)
    return y.reshape(batch, seq, -1), conv_fin[None]


def kernel(x_prompt, x_sample, state_mlstm_C, state_mlstm_n, state_mlstm_m, state_conv, norm_mix, norm_ffn, norm_final, w_mlstm_in, b_mlstm_gate, mlstm_head_norm, w_mlstm_out, w_conv_in, w_conv_taps, w_conv_out, w_ffn_gate_up, w_ffn_down):
    d = x_prompt.shape[-1]
    dk = d // N_HEADS
    assert norm_mix.shape[0] == 2 and w_mlstm_in.shape[0] == 1 and w_conv_in.shape[0] == 1
    assert w_conv_taps.shape[1] == CONV_W and 2 * N_HEADS <= LANES
    row = lambda v: v.reshape(1, -1).astype(F32)
    pad_cols = lambda m: jnp.pad(m, ((0, 0), (0, LANES - N_HEADS)))
    pad_rows = lambda m: jnp.pad(m, ((0, LANES - N_HEADS), (0, 0)))
    w_in_t = jnp.swapaxes(w_mlstm_in[0], 0, 1)
    q_scale = jnp.concatenate([jnp.full((d, 1), dk ** -0.5, F32), jnp.ones((3 * d, 1), F32)])
    wts = {
        "norm_mix0": row(norm_mix[0]), "norm_mix1": row(norm_mix[1]),
        "norm_ffn0": row(norm_ffn[0]), "norm_ffn1": row(norm_ffn[1]),
        "norm_final": row(norm_final), "g_head": row(mlstm_head_norm[0]),
        "w_in": (w_in_t[:4 * d] * q_scale).astype(BF16),
        "w_gate": jnp.concatenate([pad_rows(w_in_t[4 * d:4 * d + N_HEADS]),
                                   pad_rows(w_in_t[4 * d + N_HEADS:])], axis=0).astype(BF16),
        "b_gate": jnp.concatenate([pad_cols(b_mlstm_gate[:, :N_HEADS]),
                                   pad_cols(b_mlstm_gate[:, N_HEADS:])], axis=1).astype(F32),
        "taps": jnp.pad(w_conv_taps[0], ((0, SUBLANES - CONV_W), (0, 0))).astype(F32),
    }
    bp, sp, _ = x_prompt.shape
    bs, ss, _ = x_sample.shape
    tp, ts = bp * sp, bs * ss
    tm_p, tm_s = _token_tile(tp), _token_tile(ts)
    xp, xs = x_prompt.reshape(tp, d), x_sample.reshape(ts, d)
    n_layers, _, two_ff = w_ffn_gate_up.shape
    ff = two_ff // 2

    hg_p, c_p, n_pad, m_pad, w_gu, w_down, wts["w_conv_in"], wts["w_conv_out"], w_out0 = _mlstm_long(
        xp, wts["norm_mix0"], wts["w_in"], wts["w_gate"], wts["b_gate"], wts["g_head"], bp, sp, tm_p,
        casts=[w_ffn_gate_up.reshape(n_layers * d, two_ff), w_ffn_down.reshape(n_layers * ff, d),
               w_conv_in[0], w_conv_out[0], w_mlstm_out[0]])
    wts["w_gu"], wts["w_down"] = w_gu.reshape(n_layers, d, two_ff), w_down.reshape(n_layers, ff, d)
    n_p, m_p = n_pad[:, :N_HEADS, :], m_pad[:, 0, :N_HEADS]

    proj_s, gates_s = _inproj(xs, wts["norm_mix0"], wts["w_in"], wts["w_gate"], wts["b_gate"], tm_s)
    m0_pad = jnp.pad(state_mlstm_m[0], ((0, 0), (0, LANES - N_HEADS)))
    scan_args = (proj_s, gates_s, wts["g_head"], state_mlstm_C[0], state_mlstm_n[0], m0_pad)
    post0 = functools.partial(_post, wo=w_out0, g=wts["norm_ffn0"], wgu=wts["w_gu"], wd=wts["w_down"],
                              gfin=wts["norm_final"], layer=0, final_norm=False)
    if bs % (tp // tm_p) == 0:
        x1_p, hg_s, c_s, n_s, m_s_pad = post0(xp, hg_p, tm=tm_p, scan=scan_args + (ss,))
    else:
        x1_p, = post0(xp, hg_p, tm=tm_p)
        hg_s, c_s, n_s, m_s_pad = _scan_sample(*scan_args, ss, 4)
    x1_s, = post0(xs, hg_s, tm=tm_s)

    y_p, conv_p = _layer1(x1_p, jnp.zeros((bp, CONV_W - 1, d), F32), wts, bp, sp, tm_p)
    y_s, conv_s = _layer1(x1_s, state_conv[0], wts, bs, ss, tm_s)
    return (y_p, y_s, c_p[None], n_p[None], m_p[None], conv_p,
            c_s[None], n_s[None], m_s_pad[:, :N_HEADS][None], conv_s)
```

```python
import functools

import jax
import jax.numpy as jnp
from jax import lax
from jax.experimental import pallas as pl
from jax.experimental.pallas import tpu as pltpu

EPS = 1e-6
N_HEADS = 4
MLSTM_CHUNK = 128
CONV_W = 3
LANES = 128
SUBLANES = 8
GATE_COLS = 2 * LANES
FFN_COLS = 256
VMEM_LIMIT = 56 * 1024 * 1024

F32 = jnp.float32
BF16 = jnp.bfloat16
_NT = (((1,), (1,)), ((), ()))
_TN = (((0,), (0,)), ((), ()))


def _params(n_grid):
    return pltpu.CompilerParams(
        dimension_semantics=("arbitrary",) * n_grid, vmem_limit_bytes=VMEM_LIMIT)


def _resident(shape):
    zeros = (0,) * len(shape)
    return pl.BlockSpec(shape, lambda *_: zeros, pipeline_mode=pl.Buffered(1))


def _rmsnorm(x, g):
    return x * lax.rsqrt(jnp.mean(x * x, axis=-1, keepdims=True) + EPS) * g


def _sigmoid(x):
    return 1.0 / (1.0 + jnp.exp(-x))


def _log_sigmoid(x):
    return jnp.minimum(x, 0.0) - jnp.log1p(jnp.exp(-jnp.abs(x)))


def _prefix(x, axis, combine, identity):
    n = x.shape[axis]
    idx = lax.broadcasted_iota(jnp.int32, x.shape, axis)
    shift = 1
    while shift < n:
        x = combine(x, jnp.where(idx >= shift, pltpu.roll(x, shift, axis), identity))
        shift *= 2
    return x


def _cumsum(x, axis):
    return _prefix(x, axis, jnp.add, 0.0)


INPROJ_COLS = 256


def _inproj_pieces(x_ref, g_ref, w_ref, wg_ref, bg_ref, proj_ref, gate_ref, *, out_in):
    d = x_ref.shape[1]
    n = w_ref.shape[0] if out_in else w_ref.shape[1]
    if out_in:
        times = lambda w, cols: lax.dot_general(xn, w[cols, :], _NT, preferred_element_type=F32)
    else:
        times = lambda w, cols: jnp.dot(xn, w[:, cols], preferred_element_type=F32)
    xn = _rmsnorm(x_ref[...], g_ref[...]).astype(BF16)
    yield
    for c0 in range(0, n, INPROJ_COLS):
        cols = slice(c0, c0 + INPROJ_COLS)
        y = times(w_ref, cols)
        if c0 >= n - d:
            y = _sigmoid(y)
        proj_ref[:, cols] = y.astype(BF16)
        yield
    gate_ref[...] = times(wg_ref, slice(0, GATE_COLS)) + bg_ref[...]
    yield


def _n_inproj_pieces(n):
    return n // INPROJ_COLS + 2


def _inproj_kernel(*refs):
    for _ in _inproj_pieces(*refs, out_in=True):
        pass


def _zip_work(primary, secondary, n_primary, n_secondary):
    next(secondary, None)
    done = 1
    for k, _ in enumerate(primary, start=1):
        while done * n_primary < k * n_secondary:
            next(secondary, None)
            done += 1
    for _ in secondary:
        pass


def _cast_specs(arr, n_steps):
    rows, cols = arr.shape
    pack = 2 * SUBLANES
    nb = max(n for n in range(1, n_steps + 1) if rows % n == 0 and (rows // n) % pack == 0)
    blk = lambda i: (jnp.minimum(i * nb // n_steps, nb - 1), 0)
    spec = pl.BlockSpec((rows // nb, cols), blk)
    return spec, spec, jax.ShapeDtypeStruct((rows, cols), BF16)


def _inproj(x, g, w, wg, bg, tm):
    t, d = x.shape
    n = w.shape[0]
    return pl.pallas_call(
        _inproj_kernel,
        grid=(t // tm,),
        in_specs=[pl.BlockSpec((tm, d), lambda i: (i, 0)),
                  _resident((1, d)), _resident((n, d)), _resident((GATE_COLS, d)), _resident((1, GATE_COLS))],
        out_specs=[pl.BlockSpec((tm, n), lambda i: (i, 0)),
                   pl.BlockSpec((tm, GATE_COLS), lambda i: (i, 0))],
        out_shape=[jax.ShapeDtypeStruct((t, n), BF16), jax.ShapeDtypeStruct((t, GATE_COLS), F32)],
        compiler_params=_params(1),
        name="mlstm_inproj",
    )(x, g, w, wg, bg)


def _gated_head_out(h, g_head, sig_o):
    hn = h * lax.rsqrt(jnp.mean(h * h, axis=-1, keepdims=True) + EPS)
    return hn * g_head * sig_o


def _per_row(rep, width):
    return jnp.concatenate([rep] * (width // LANES), axis=1)


def _scan_chunk(proj_ref, gate_ref, gh_ref, hg_ref, c_ref, n_ref, m_ref, r0):
    seq = MLSTM_CHUNK
    d = gh_ref.shape[1]
    dk = d // N_HEADS
    assert seq == LANES
    rows = slice(r0, r0 + seq)

    lane = lax.broadcasted_iota(jnp.int32, (1, LANES), 1)
    li = gate_ref[rows, 0:LANES]
    b = _prefix(_log_sigmoid(gate_ref[rows, LANES:GATE_COLS]), 0, jnp.add, 0.0)
    a = li - b
    m_prev = m_ref[0, 0:1, :]
    neg_mx = -jnp.maximum(m_prev, _prefix(a, 0, jnp.maximum, -jnp.inf))
    exp_neg_meff = jnp.exp(neg_mx - b)
    b_last = b[seq - 1:seq, :]
    g = b_last - b + li
    m_new = jnp.maximum(b_last + m_prev, jnp.max(g, axis=0, keepdims=True))
    decay_row = jnp.exp(b_last + m_prev - m_new)
    wg_t = jnp.exp(g - m_new).T
    a_t = a.T
    m_ref[0, 0:1, :] = jnp.where(lane < N_HEADS, m_new, 0.0)

    causal = (lax.broadcasted_iota(jnp.int32, (seq, seq), 0)
              >= lax.broadcasted_iota(jnp.int32, (seq, seq), 1))
    wg16 = wg_t[0:2 * SUBLANES, :].astype(BF16)
    den = jnp.zeros((seq, LANES), F32)
    heads = range(N_HEADS)
    tok = lambda j, h: proj_ref[rows, j * d + h * dk:j * d + (h + 1) * dk]
    yield
    p, w_rep, qx, upd = [], [], [], []
    for h in heads:
        neg_mx_rep = jnp.broadcast_to(neg_mx[:, h:h + 1], (seq, LANES))
        p.append(jnp.exp(jnp.where(causal, neg_mx_rep + a_t[h:h + 1, :], -jnp.inf)))
        w_rep.append(jnp.exp(neg_mx_rep + m_prev[:, h:h + 1]))
        rhs = jnp.concatenate([tok(1, h), c_ref[0, h].astype(BF16),
                               jnp.broadcast_to(n_ref[0, h:h + 1, :].astype(BF16), (LANES, dk))], axis=0)
        qx.append(lax.dot_general(tok(0, h), rhs, _NT, preferred_element_type=F32))
        vtw = (tok(2, h).T.astype(F32) * wg_t[h:h + 1, :]).astype(BF16)
        upd.append(jnp.dot(jnp.concatenate([vtw, wg16], axis=0), tok(1, h), preferred_element_type=F32))
        yield
    sv, s_sum = [], []
    for h in heads:
        s = qx[h][:, :seq] * p[h]
        s_sum.append(jnp.sum(s, axis=1, keepdims=True))
        sv.append(jnp.dot(s.astype(BF16), tok(2, h), preferred_element_type=F32))
        yield
    nums = []
    for h in heads:
        nums.append(_per_row(w_rep[h], dk) * qx[h][:, seq:seq + dk] + sv[h])
        den = jnp.where(lane == h, w_rep[h] * qx[h][:, seq + dk:] + s_sum[h], den)
        decay = decay_row[:, h:h + 1]
        c_ref[0, h] = decay * c_ref[0, h] + upd[h][:dk]
        n_ref[0, h:h + 1, :] = decay * n_ref[0, h:h + 1, :] + upd[h][dk + h:dk + h + 1, :]
        yield
    recip = 1.0 / jnp.maximum(jnp.abs(den), exp_neg_meff)
    yield
    hh, ssq_rep = [], []
    for h in heads:
        hh.append(nums[h] * _per_row(jnp.broadcast_to(recip[:, h:h + 1], (seq, LANES)), dk))
        ssq_rep.append(jnp.broadcast_to(jnp.sum(hh[h] * hh[h], axis=1, keepdims=True), (seq, LANES)))
        yield
    for h in heads:
        cols = slice(h * dk, (h + 1) * dk)
        hn = hh[h] * _per_row(lax.rsqrt(ssq_rep[h] * (1.0 / dk) + EPS), dk)
        hg_ref[rows, cols] = (hn * gh_ref[:, cols] * tok(3, h).astype(F32)).astype(BF16)
        yield


SCAN_CHUNK_PIECES = 5 * N_HEADS + 2


def _mlstm_long_kernel(*refs, tiles_per_seq, n_cast):
    x_ref, g_ref, w_ref, wg_ref, bg_ref, gh_ref = refs[:6]
    cast_src = refs[6:6 + n_cast]
    hg_ref, c_ref, n_ref, m_ref = refs[6 + n_cast:10 + n_cast]
    cast_dst = refs[10 + n_cast:10 + 2 * n_cast]
    proj_a, gate_a, proj_b, gate_b, w_io, wg_io = refs[10 + 2 * n_cast:]
    i = pl.program_id(0)
    tm = x_ref.shape[0]

    @pl.when(i == 0)
    def _():
        proj_b[...] = jnp.zeros_like(proj_b)
        gate_b[...] = jnp.zeros_like(gate_b)
        for c0 in range(0, w_ref.shape[0], INPROJ_COLS):
            w_io[:, c0:c0 + INPROJ_COLS] = w_ref[c0:c0 + INPROJ_COLS, :].T
        wg_io[...] = wg_ref[...].T

    @pl.when(lax.rem(jnp.maximum(i - 1, 0), tiles_per_seq) == 0)
    def _():
        c_ref[...] = jnp.zeros_like(c_ref)
        n_ref[...] = jnp.zeros_like(n_ref)
        m_ref[...] = jnp.zeros_like(m_ref)

    n_chunks = tm // MLSTM_CHUNK

    def scan_tile(proj_r, gate_r):
        for c in range(n_chunks):
            yield from _scan_chunk(proj_r, gate_r, gh_ref, hg_ref, c_ref, n_ref, m_ref, c * MLSTM_CHUNK)

    def project_and_cast(proj_w, gate_w):
        casts = list(zip(cast_src, cast_dst))
        for k, _ in enumerate(_inproj_pieces(x_ref, g_ref, w_io, wg_io, bg_ref, proj_w, gate_w,
                                             out_in=False)):
            if casts and k % 3 == 1:
                src, dst = casts.pop()
                dst[...] = src[...].astype(BF16)
            yield
        for src, dst in casts:
            dst[...] = src[...].astype(BF16)

    def step(proj_w, gate_w, proj_r, gate_r):
        _zip_work(scan_tile(proj_r, gate_r), project_and_cast(proj_w, gate_w),
                  n_chunks * SCAN_CHUNK_PIECES, _n_inproj_pieces(w_ref.shape[0]))

    @pl.when(lax.rem(i, 2) == 0)
    def _():
        step(proj_a, gate_a, proj_b, gate_b)

    @pl.when(lax.rem(i, 2) == 1)
    def _():
        step(proj_b, gate_b, proj_a, gate_a)


def _mlstm_long(x, g, w, wg, bg, g_head, batch, seq, tm, casts):
    t, d = x.shape
    n = w.shape[0]
    dk = d // N_HEADS
    n_tiles = t // tm
    tiles_per_seq = seq // tm
    assert seq % tm == 0 and tm % MLSTM_CHUNK == 0
    prev = lambda i: jnp.maximum(i - 1, 0)
    cast_in, cast_out, cast_shape = zip(*[_cast_specs(a, n_tiles + 1) for a in casts])
    return pl.pallas_call(
        functools.partial(_mlstm_long_kernel, tiles_per_seq=tiles_per_seq, n_cast=len(casts)),
        grid=(n_tiles + 1,),
        in_specs=[pl.BlockSpec((tm, d), lambda i: (jnp.minimum(i, n_tiles - 1), 0)),
                  _resident((1, d)), _resident((n, d)), _resident((GATE_COLS, d)),
                  _resident((1, GATE_COLS)), _resident((1, d)), *cast_in],
        out_specs=[pl.BlockSpec((tm, d), lambda i: (prev(i), 0)),
                   pl.BlockSpec((1, N_HEADS, dk, dk), lambda i: (prev(i) // tiles_per_seq, 0, 0, 0)),
                   pl.BlockSpec((1, SUBLANES, dk), lambda i: (prev(i) // tiles_per_seq, 0, 0)),
                   pl.BlockSpec((1, SUBLANES, LANES), lambda i: (prev(i) // tiles_per_seq, 0, 0)),
                   *cast_out],
        out_shape=[jax.ShapeDtypeStruct((t, d), BF16),
                   jax.ShapeDtypeStruct((batch, N_HEADS, dk, dk), F32),
                   jax.ShapeDtypeStruct((batch, SUBLANES, dk), F32),
                   jax.ShapeDtypeStruct((batch, SUBLANES, LANES), F32), *cast_shape],
        scratch_shapes=[pltpu.VMEM((tm, n), BF16), pltpu.VMEM((tm, GATE_COLS), F32),
                        pltpu.VMEM((tm, n), BF16), pltpu.VMEM((tm, GATE_COLS), F32),
                        pltpu.VMEM((d, n), BF16), pltpu.VMEM((d, GATE_COLS), BF16)],
        compiler_params=_params(1),
        name="mlstm_long",
    )(x, g, w, wg, bg, g_head, *casts)


SCAN_SAMPLE_GROUP = 4


def _n_scan_sample_pieces(bt):
    return 1 + 4 * (bt * N_HEADS // SCAN_SAMPLE_GROUP)


def _scan_sample_pieces(q_ref, k_ref, v_ref, so_ref, gate_ref, gh_ref, c_in, n_in, m_in,
                        hg_ref, c_out, n_out, m_out, *, seq):
    bt = c_in.shape[0]
    dk = q_ref.shape[1] // N_HEADS
    step = pl.program_id(0)
    row = lax.broadcasted_iota(jnp.int32, (seq, seq), 0)
    col = lax.broadcasted_iota(jnp.int32, (seq, seq), 1)
    causal = row >= col
    diag = row == col
    lane = lax.broadcasted_iota(jnp.int32, (1, LANES), 1)
    to_row = lambda x_col: jnp.sum(jnp.where(diag, x_col, 0.0), axis=0, keepdims=True)

    qs, ks, vs = q_ref[...].astype(F32), k_ref[...].astype(F32), v_ref[...].astype(F32)
    sig_o = so_ref[...].astype(F32)
    pairs = [(j, h) for j in range(bt) for h in range(N_HEADS)]
    tok = lambda x, j, h: x[j * seq:(j + 1) * seq, h * dk:(h + 1) * dk]

    gate_q = {}
    for j in range(bt):
        gates = gate_ref[j * seq:(j + 1) * seq, :]
        b_col = _cumsum(_log_sigmoid(gates), 0)
        m_prev_row = m_in[pl.ds(step * bt + j, 1), :]
        m_new_row = jnp.zeros((1, LANES), F32)
        for h in range(N_HEADS):
            li_c, b_c = gates[:, h:h + 1], b_col[:, LANES + h:LANES + h + 1]
            m_prev = m_prev_row[:, h:h + 1]
            b_last = b_c[seq - 1:seq, :]
            dm = jnp.where(causal, b_c - to_row(b_c) + to_row(li_c), -jnp.inf)
            inter = b_c + m_prev
            m_eff = jnp.maximum(inter, jnp.max(dm, axis=1, keepdims=True))
            g = b_last - b_c + li_c
            m_new = jnp.maximum(b_last + m_prev, jnp.max(g, axis=0, keepdims=True))
            gate_q[j, h] = dict(w_inter=jnp.exp(inter - m_eff), p=jnp.exp(dm - m_eff),
                                floor=jnp.exp(-m_eff), decay=jnp.exp(b_last + m_prev - m_new),
                                wg=jnp.exp(g - m_new))
            m_new_row = jnp.where(lane == h, m_new, m_new_row)
        m_out[pl.ds(step * bt + j, 1), :] = m_new_row
    yield

    def grouped(seq_of_pairs):
        for idx, pair in enumerate(seq_of_pairs, start=1):
            yield pair, idx % SCAN_SAMPLE_GROUP == 0

    pack = 2 * SUBLANES
    assert seq <= pack
    as_tile = lambda a: jnp.concatenate(
        [a, jnp.zeros((pack - a.shape[0], a.shape[1]), F32)], axis=0).astype(BF16)
    qx = {}
    for (j, h), pause in grouped(pairs):
        rhs = jnp.concatenate([c_in[j, h].astype(BF16), as_tile(tok(ks, j, h)),
                               as_tile(n_in[j, h:h + 1, :])], axis=0)
        qx[j, h] = lax.dot_general(as_tile(tok(qs, j, h)), rhs, _NT,
                                   preferred_element_type=F32)[:seq]
        if pause:
            yield

    ones = jnp.ones((seq, LANES), F32)
    sv = {}
    for (j, h), pause in grouped(pairs):
        s = qx[j, h][:, dk:dk + seq] * gate_q[j, h]["p"]
        sv[j, h] = jnp.dot(s, jnp.concatenate([tok(vs, j, h), ones], axis=1),
                           preferred_element_type=F32)
        if pause:
            yield

    for (j, h), pause in grouped(pairs):
        gq = gate_q[j, h]
        num = gq["w_inter"] * qx[j, h][:, :dk] + sv[j, h][:, :dk]
        den = gq["w_inter"] * qx[j, h][:, dk + pack:dk + pack + 1] + sv[j, h][:, dk:dk + 1]
        hh = num * (1.0 / jnp.maximum(jnp.abs(den), gq["floor"]))
        hg_ref[j * seq:(j + 1) * seq, h * dk:(h + 1) * dk] = _gated_head_out(
            hh, gh_ref[:, h * dk:(h + 1) * dk], tok(sig_o, j, h))
        if pause:
            yield

    def update_of(j, h):
        return lax.dot_general(as_tile(tok(vs, j, h) * gate_q[j, h]["wg"]), as_tile(tok(ks, j, h)),
                               _TN, preferred_element_type=F32)

    upd_next = update_of(*pairs[0])
    for idx, ((j, h), pause) in enumerate(grouped(pairs)):
        upd = upd_next
        if idx + 1 < len(pairs):
            upd_next = update_of(*pairs[idx + 1])
        gq = gate_q[j, h]
        c_out[j, h] = gq["decay"] * c_in[j, h] + upd
        n_out[j, h:h + 1, :] = (gq["decay"] * n_in[j, h:h + 1, :]
                                + jnp.sum(tok(ks, j, h) * gq["wg"], axis=0, keepdims=True))
        if pause:
            yield


def _scan_sample_kernel(*refs, seq):
    for _ in _scan_sample_pieces(*refs, seq=seq):
        pass


def _scan_sample_specs(g_head, c0, n0, seq, bt):
    batch = c0.shape[0]
    d = g_head.shape[1]
    dk = d // N_HEADS
    tok = lambda j: pl.BlockSpec((bt * seq, d), lambda i: (i, j))
    c_spec = pl.BlockSpec((bt, N_HEADS, dk, dk), lambda i: (i, 0, 0, 0))
    n_spec = pl.BlockSpec((bt, N_HEADS, dk), lambda i: (i, 0, 0))
    m_spec = pl.BlockSpec((batch, LANES), lambda i: (0, 0))
    in_specs = [tok(0), tok(1), tok(2), tok(3),
                pl.BlockSpec((bt * seq, GATE_COLS), lambda i: (i, 0)),
                _resident((1, d)), c_spec, n_spec, m_spec]
    out_shapes = [jax.ShapeDtypeStruct((batch * seq, d), F32),
                  jax.ShapeDtypeStruct(c0.shape, F32),
                  jax.ShapeDtypeStruct(n0.shape, F32),
                  jax.ShapeDtypeStruct((batch, LANES), F32)]
    return in_specs, [tok(0), c_spec, n_spec, m_spec], out_shapes


def _scan_sample(proj, gates, g_head, c0, n0, m0, seq, bt):
    in_specs, out_specs, out_shapes = _scan_sample_specs(g_head, c0, n0, seq, bt)
    return pl.pallas_call(
        functools.partial(_scan_sample_kernel, seq=seq),
        grid=(c0.shape[0] // bt,),
        in_specs=in_specs, out_specs=out_specs, out_shape=out_shapes,
        compiler_params=_params(1),
        name="mlstm_scan_sample",
    )(proj, proj, proj, proj, gates, g_head, c0, n0, m0)


def _post_pieces(x_ref, z_ref, wo_ref, g_ref, wgu_ref, wd_ref, gfin_ref, out_ref,
                 xn_ref, act_ref, *, final_norm):
    x1 = x_ref[...] + jnp.dot(z_ref[...].astype(BF16), wo_ref[...], preferred_element_type=F32)
    out_ref[...] = x1
    xn_ref[...] = _rmsnorm(x1, g_ref[...]).astype(BF16)
    yield
    ff = wd_ref.shape[0]
    for c in range(ff // FFN_COLS):
        gate = jnp.dot(xn_ref[...], wgu_ref[:, c * FFN_COLS:(c + 1) * FFN_COLS],
                       preferred_element_type=F32)
        up = jnp.dot(xn_ref[...], wgu_ref[:, ff + c * FFN_COLS:ff + (c + 1) * FFN_COLS],
                     preferred_element_type=F32)
        act_ref[:, c * FFN_COLS:(c + 1) * FFN_COLS] = (gate * _sigmoid(gate) * up).astype(BF16)
        yield
    x2 = out_ref[...] + jnp.dot(act_ref[...], wd_ref[...], preferred_element_type=F32)
    if final_norm:
        x2 = _rmsnorm(x2, gfin_ref[...])
    out_ref[...] = x2
    yield


def _n_post_pieces(ff):
    return ff // FFN_COLS + 2


def _post_kernel(*refs, final_norm, scan_seq):
    if scan_seq is None:
        for _ in _post_pieces(*refs, final_norm=final_norm):
            pass
        return
    post_in, scan_in = refs[:7], refs[7:16]
    out_ref, scan_out = refs[16], refs[17:21]
    xn_ref, act_ref = refs[21:]
    _zip_work(_post_pieces(*post_in, out_ref, xn_ref, act_ref, final_norm=final_norm),
              _scan_sample_pieces(*scan_in, *scan_out, seq=scan_seq),
              _n_post_pieces(post_in[5].shape[0]), _n_scan_sample_pieces(scan_in[6].shape[0]))


def _post(x, z, wo, g, wgu, wd, gfin, layer, tm, final_norm, scan=None):
    t, d = x.shape
    ff = wd.shape[1]
    assert ff % FFN_COLS == 0
    n_steps = t // tm
    tile = pl.BlockSpec((tm, d), lambda i: (i, 0))
    of_layer = lambda rows, cols: pl.BlockSpec((None, rows, cols), lambda i: (layer, 0, 0),
                                               pipeline_mode=pl.Buffered(1))
    in_specs = [tile, tile, _resident((d, d)), _resident((1, d)),
                of_layer(d, 2 * ff), of_layer(ff, d), _resident((1, d))]
    out_specs, out_shapes, operands = [tile], [jax.ShapeDtypeStruct((t, d), F32)], [x, z, wo, g, wgu, wd, gfin]
    scan_seq = None
    if scan is not None:
        proj, gates, g_head, c0, n0, m0, scan_seq = scan
        assert c0.shape[0] % n_steps == 0
        s_in, s_out, s_shapes = _scan_sample_specs(g_head, c0, n0, scan_seq, c0.shape[0] // n_steps)
        in_specs, out_specs, out_shapes = in_specs + s_in, out_specs + s_out, out_shapes + s_shapes
        operands += [proj, proj, proj, proj, gates, g_head, c0, n0, m0]
    return pl.pallas_call(
        functools.partial(_post_kernel, final_norm=final_norm, scan_seq=scan_seq),
        grid=(n_steps,),
        in_specs=in_specs, out_specs=out_specs, out_shape=out_shapes,
        scratch_shapes=[pltpu.VMEM((tm, d), BF16), pltpu.VMEM((tm, ff), BF16)],
        compiler_params=_params(1),
        name="outproj_ffn_final" if final_norm else "outproj_ffn",
    )(*operands)


def _conv_proj(x_ref, g_ref, w_ref):
    d = x_ref.shape[1]
    xn = _rmsnorm(x_ref[...], g_ref[...]).astype(BF16)
    part = lambda c: jnp.dot(xn, w_ref[:, c * d:(c + 1) * d], preferred_element_type=F32)
    return part(0), part(1) * part(2)


def _conv_long_pieces(x_ref, g_ref, w_ref, taps_ref, st_out_ref, carry_ref, z_ref, x_copy_ref):
    tm, d = x_ref.shape
    x = x_ref[...]
    x_copy_ref[...] = x
    xn = _rmsnorm(x, g_ref[...]).astype(BF16)
    row = lax.broadcasted_iota(jnp.int32, (tm, 1), 0)
    yield
    for c0 in range(0, d, INPROJ_COLS):
        cols = slice(c0, c0 + INPROJ_COLS)
        part = lambda k: jnp.dot(xn, w_ref[:, k * d + c0:k * d + c0 + INPROJ_COLS],
                                 preferred_element_type=F32)
        bg, u = part(0), part(1) * part(2)
        yield
        s0, s1 = carry_ref[0:1, cols], carry_ref[1:2, cols]
        prev1 = jnp.where(row == 0, s1, pltpu.roll(u, 1, 0))
        prev2 = jnp.where(row == 0, s0, jnp.where(row == 1, s1, pltpu.roll(u, 2, 0)))
        conv = taps_ref[0:1, cols] * prev2 + taps_ref[1:2, cols] * prev1 + taps_ref[2:3, cols] * u
        z_ref[:, cols] = (bg * conv).astype(BF16)
        tail = u[tm - (CONV_W - 1):tm, :]
        carry_ref[:, cols] = tail
        st_out_ref[0, :, cols] = tail
        yield


def _n_conv_long_pieces(d):
    return 1 + 2 * (d // INPROJ_COLS)


def _layer1_long_kernel(x_ref, gmix_ref, wcin_ref, taps_ref, st_ref, wo_ref, gffn_ref, wgu_ref,
                        wd_ref, gfin_ref, out_ref, st_out_ref,
                        carry_ref, z_a, x_a, z_b, x_b, xn_ref, act_ref, *, tiles_per_seq, n_tiles):
    i = pl.program_id(0)

    @pl.when(i == 0)
    def _():
        z_b[...] = jnp.zeros_like(z_b)
        x_b[...] = jnp.zeros_like(x_b)

    @pl.when(jnp.logical_and(i < n_tiles, lax.rem(i, tiles_per_seq) == 0))
    def _():
        carry_ref[...] = st_ref[0]

    def step(z_w, x_w, z_r, x_r):
        _zip_work(_post_pieces(x_r, z_r, wo_ref, gffn_ref, wgu_ref, wd_ref, gfin_ref, out_ref,
                               xn_ref, act_ref, final_norm=True),
                  _conv_long_pieces(x_ref, gmix_ref, wcin_ref, taps_ref, st_out_ref, carry_ref, z_w, x_w),
                  _n_post_pieces(wd_ref.shape[0]), _n_conv_long_pieces(x_ref.shape[1]))

    @pl.when(lax.rem(i, 2) == 0)
    def _():
        step(z_a, x_a, z_b, x_b)

    @pl.when(lax.rem(i, 2) == 1)
    def _():
        step(z_b, x_b, z_a, x_a)


def _layer1_long(x, gmix, wcin, taps, state, wo, gffn, wgu, wd, gfin, layer, batch, seq, tm):
    t, d = x.shape
    ff = wd.shape[1]
    keep = CONV_W - 1
    n_tiles = t // tm
    tiles_per_seq = seq // tm
    assert seq % tm == 0
    cur = lambda i: jnp.minimum(i, n_tiles - 1)
    st_spec = pl.BlockSpec((1, keep, d), lambda i: (cur(i) // tiles_per_seq, 0, 0))
    of_layer = lambda rows, cols: pl.BlockSpec((None, rows, cols), lambda i: (layer, 0, 0),
                                               pipeline_mode=pl.Buffered(1))
    return pl.pallas_call(
        functools.partial(_layer1_long_kernel, tiles_per_seq=tiles_per_seq, n_tiles=n_tiles),
        grid=(n_tiles + 1,),
        in_specs=[pl.BlockSpec((tm, d), lambda i: (cur(i), 0)),
                  _resident((1, d)), _resident((d, 3 * d)), _resident((SUBLANES, d)), st_spec,
                  _resident((d, d)), _resident((1, d)), of_layer(d, 2 * ff), of_layer(ff, d),
                  _resident((1, d))],
        out_specs=[pl.BlockSpec((tm, d), lambda i: (jnp.maximum(i - 1, 0), 0)), st_spec],
        out_shape=[jax.ShapeDtypeStruct((t, d), F32), jax.ShapeDtypeStruct((batch, keep, d), F32)],
        scratch_shapes=[pltpu.VMEM((keep, d), F32),
                        pltpu.VMEM((tm, d), BF16), pltpu.VMEM((tm, d), F32),
                        pltpu.VMEM((tm, d), BF16), pltpu.VMEM((tm, d), F32),
                        pltpu.VMEM((tm, d), BF16), pltpu.VMEM((tm, ff), BF16)],
        compiler_params=_params(1),
        name="layer1_long",
    )(x, gmix, wcin, taps, state, wo, gffn, wgu, wd, gfin)


def _conv_short_kernel(x_ref, g_ref, w_ref, taps_ref, st_ref, z_ref, st_out_ref, *, seq):
    tm, d = x_ref.shape
    nseq = tm // seq
    bg, u = _conv_proj(x_ref, g_ref, w_ref)
    pos = lax.broadcasted_iota(jnp.int32, (1, seq, 1), 1)
    s0, s1 = st_ref[:, 0:1, :], st_ref[:, 1:2, :]
    u3 = u.reshape(nseq, seq, d)
    r1 = pltpu.roll(u, 1, 0).reshape(nseq, seq, d)
    r2 = pltpu.roll(u, 2, 0).reshape(nseq, seq, d)
    prev1 = jnp.where(pos == 0, s1, r1)
    prev2 = jnp.where(pos == 0, s0, jnp.where(pos == 1, s1, r2))
    conv = (taps_ref[0:1, :].reshape(1, 1, d) * prev2 + taps_ref[1:2, :].reshape(1, 1, d) * prev1
            + taps_ref[2:3, :].reshape(1, 1, d) * u3)
    z_ref[...] = (bg * conv.reshape(tm, d)).astype(BF16)
    st_out_ref[...] = u3[:, seq - (CONV_W - 1):seq, :]


def _conv_short(x, g, w, taps, state, batch, seq, tm):
    t, d = x.shape
    keep = CONV_W - 1
    assert tm % seq == 0
    out_shape = [jax.ShapeDtypeStruct((t, d), BF16), jax.ShapeDtypeStruct((batch, keep, d), F32)]
    weights = [_resident((1, d)), _resident((d, 3 * d)), _resident((SUBLANES, d))]
    nseq = tm // seq
    st_spec = pl.BlockSpec((nseq, keep, d), lambda i: (i, 0, 0))
    tile = pl.BlockSpec((tm, d), lambda i: (i, 0))
    return pl.pallas_call(
        functools.partial(_conv_short_kernel, seq=seq),
        grid=(t // tm,),
        in_specs=[tile] + weights + [st_spec],
        out_specs=[tile, st_spec],
        out_shape=out_shape,
        compiler_params=_params(1),
        name="gated_conv_short",
    )(x, g, w, taps, state)


def _token_tile(t):
    return min(t, 512)


def _layer1(x1, conv0, wts, batch, seq, tm):
    if seq >= tm:
        y, conv_fin = _layer1_long(x1, wts["norm_mix1"], wts["w_conv_in"], wts["taps"], conv0,
                                   wts["w_conv_out"], wts["norm_ffn1"], wts["w_gu"], wts["w_down"],
                                   wts["norm_final"], 1, batch, seq, tm)
    else:
        z, conv_fin = _conv_short(x1, wts["norm_mix1"], wts["w_conv_in"], wts["taps"], conv0,
                                  batch, seq, tm)
        y, = _post(x1, z, wts["w_conv_out"], wts["norm_ffn1"], wts["w_gu"], wts["w_down"],
                   wts["norm_final"], 1, tm, final_norm=True)
    return y.reshape(batch, seq, -1), conv_fin[None]


def kernel(x_prompt, x_sample, state_mlstm_C, state_mlstm_n, state_mlstm_m, state_conv, norm_mix, norm_ffn, norm_final, w_mlstm_in, b_mlstm_gate, mlstm_head_norm, w_mlstm_out, w_conv_in, w_conv_taps, w_conv_out, w_ffn_gate_up, w_ffn_down):
    d = x_prompt.shape[-1]
    dk = d // N_HEADS
    assert norm_mix.shape[0] == 2 and w_mlstm_in.shape[0] == 1 and w_conv_in.shape[0] == 1
    assert w_conv_taps.shape[1] == CONV_W and 2 * N_HEADS <= LANES
    row = lambda v: v.reshape(1, -1).astype(F32)
    pad_cols = lambda m: jnp.pad(m, ((0, 0), (0, LANES - N_HEADS)))
    pad_rows = lambda m: jnp.pad(m, ((0, LANES - N_HEADS), (0, 0)))
    w_in_t = jnp.swapaxes(w_mlstm_in[0], 0, 1)
    q_scale = jnp.concatenate([jnp.full((d, 1), dk ** -0.5, F32), jnp.ones((3 * d, 1), F32)])
    wts = {
        "norm_mix0": row(norm_mix[0]), "norm_mix1": row(norm_mix[1]),
        "norm_ffn0": row(norm_ffn[0]), "norm_ffn1": row(norm_ffn[1]),
        "norm_final": row(norm_final), "g_head": row(mlstm_head_norm[0]),
        "w_in": (w_in_t[:4 * d] * q_scale).astype(BF16),
        "w_gate": jnp.concatenate([pad_rows(w_in_t[4 * d:4 * d + N_HEADS]),
                                   pad_rows(w_in_t[4 * d + N_HEADS:])], axis=0).astype(BF16),
        "b_gate": jnp.concatenate([pad_cols(b_mlstm_gate[:, :N_HEADS]),
                                   pad_cols(b_mlstm_gate[:, N_HEADS:])], axis=1).astype(F32),
        "taps": jnp.pad(w_conv_taps[0], ((0, SUBLANES - CONV_W), (0, 0))).astype(F32),
    }
    bp, sp, _ = x_prompt.shape
    bs, ss, _ = x_sample.shape
    tp, ts = bp * sp, bs * ss
    tm_p, tm_s = _token_tile(tp), _token_tile(ts)
    xp, xs = x_prompt.reshape(tp, d), x_sample.reshape(ts, d)
    n_layers, _, two_ff = w_ffn_gate_up.shape
    ff = two_ff // 2

    hg_p, c_p, n_pad, m_pad, w_gu, w_down, wts["w_conv_in"], wts["w_conv_out"], w_out0 = _mlstm_long(
        xp, wts["norm_mix0"], wts["w_in"], wts["w_gate"], wts["b_gate"], wts["g_head"], bp, sp, tm_p,
        casts=[w_ffn_gate_up.reshape(n_layers * d, two_ff), w_ffn_down.reshape(n_layers * ff, d),
               w_conv_in[0], w_conv_out[0], w_mlstm_out[0]])
    wts["w_gu"], wts["w_down"] = w_gu.reshape(n_layers, d, two_ff), w_down.reshape(n_layers, ff, d)
    n_p, m_p = n_pad[:, :N_HEADS, :], m_pad[:, 0, :N_HEADS]

    proj_s, gates_s = _inproj(xs, wts["norm_mix0"], wts["w_in"], wts["w_gate"], wts["b_gate"], tm_s)
    m0_pad = jnp.pad(state_mlstm_m[0], ((0, 0), (0, LANES - N_HEADS)))
    scan_args = (proj_s, gates_s, wts["g_head"], state_mlstm_C[0], state_mlstm_n[0], m0_pad)
    post0 = functools.partial(_post, wo=w_out0, g=wts["norm_ffn0"], wgu=wts["w_gu"], wd=wts["w_down"],
                              gfin=wts["norm_final"], layer=0, final_norm=False)
    if bs % (tp // tm_p) == 0:
        x1_p, hg_s, c_s, n_s, m_s_pad = post0(xp, hg_p, tm=tm_p, scan=scan_args + (ss,))
    else:
        x1_p, = post0(xp, hg_p, tm=tm_p)
        hg_s, c_s, n_s, m_s_pad = _scan_sample(*scan_args, ss, 4)
    x1_s, = post0(xs, hg_s, tm=tm_s)

    y_p, conv_p = _layer1(x1_p, jnp.zeros((bp, CONV_W - 1, d), F32), wts, bp, sp, tm_p)
    y_s, conv_s = _layer1(x1_s, state_conv[0], wts, bs, ss, tm_s)
    return (y_p, y_s, c_p[None], n_p[None], m_p[None], conv_p,
            c_s[None], n_s[None], m_s_pad[:, :N_HEADS][None], conv_s)
```

```python
import functools

import jax
import jax.numpy as jnp
from jax import lax
from jax.experimental import pallas as pl
from jax.experimental.pallas import tpu as pltpu

EPS = 1e-6
N_HEADS = 4
MLSTM_CHUNK = 128
CONV_W = 3
LANES = 128
SUBLANES = 8
GATE_COLS = 2 * LANES
FFN_COLS = 256
VMEM_LIMIT = 56 * 1024 * 1024

F32 = jnp.float32
BF16 = jnp.bfloat16
_NT = (((1,), (1,)), ((), ()))
_TN = (((0,), (0,)), ((), ()))


def _params(n_grid):
    return pltpu.CompilerParams(
        dimension_semantics=("arbitrary",) * n_grid, vmem_limit_bytes=VMEM_LIMIT)


def _resident(shape):
    zeros = (0,) * len(shape)
    return pl.BlockSpec(shape, lambda *_: zeros, pipeline_mode=pl.Buffered(1))


def _rmsnorm(x, g):
    return x * lax.rsqrt(jnp.mean(x * x, axis=-1, keepdims=True) + EPS) * g


def _sigmoid(x):
    return 1.0 / (1.0 + jnp.exp(-x))


def _log_sigmoid(x):
    return jnp.minimum(x, 0.0) - jnp.log1p(jnp.exp(-jnp.abs(x)))


def _prefix(x, axis, combine, identity):
    n = x.shape[axis]
    idx = lax.broadcasted_iota(jnp.int32, x.shape, axis)
    shift = 1
    while shift < n:
        x = combine(x, jnp.where(idx >= shift, pltpu.roll(x, shift, axis), identity))
        shift *= 2
    return x


def _cumsum(x, axis):
    return _prefix(x, axis, jnp.add, 0.0)


INPROJ_COLS = 256


def _inproj_pieces(x_ref, g_ref, w_ref, wg_ref, bg_ref, proj_ref, gate_ref, *, out_in):
    d = x_ref.shape[1]
    n = w_ref.shape[0] if out_in else w_ref.shape[1]
    if out_in:
        times = lambda w, cols: lax.dot_general(xn, w[cols, :], _NT, preferred_element_type=F32)
    else:
        times = lambda w, cols: jnp.dot(xn, w[:, cols], preferred_element_type=F32)
    xn = _rmsnorm(x_ref[...], g_ref[...]).astype(BF16)
    yield
    for c0 in range(0, n, INPROJ_COLS):
        cols = slice(c0, c0 + INPROJ_COLS)
        y = times(w_ref, cols)
        if c0 >= n - d:
            y = _sigmoid(y)
        proj_ref[:, cols] = y.astype(BF16)
        yield
    gate_ref[...] = times(wg_ref, slice(0, GATE_COLS)) + bg_ref[...]
    yield


def _n_inproj_pieces(n):
    return n // INPROJ_COLS + 2


def _inproj_kernel(*refs):
    for _ in _inproj_pieces(*refs, out_in=True):
        pass


def _zip_work(primary, secondary, n_primary, n_secondary):
    next(secondary, None)
    done = 1
    for k, _ in enumerate(primary, start=1):
        while done * n_primary < k * n_secondary:
            next(secondary, None)
            done += 1
    for _ in secondary:
        pass


def _cast_specs(arr, n_steps):
    rows, cols = arr.shape
    pack = 2 * SUBLANES
    nb = max(n for n in range(1, n_steps + 1) if rows % n == 0 and (rows // n) % pack == 0)
    blk = lambda i: (jnp.minimum(i * nb // n_steps, nb - 1), 0)
    spec = pl.BlockSpec((rows // nb, cols), blk)
    return spec, spec, jax.ShapeDtypeStruct((rows, cols), BF16)


def _inproj(x, g, w, wg, bg, tm):
    t, d = x.shape
    n = w.shape[0]
    return pl.pallas_call(
        _inproj_kernel,
        grid=(t // tm,),
        in_specs=[pl.BlockSpec((tm, d), lambda i: (i, 0)),
                  _resident((1, d)), _resident((n, d)), _resident((GATE_COLS, d)), _resident((1, GATE_COLS))],
        out_specs=[pl.BlockSpec((tm, n), lambda i: (i, 0)),
                   pl.BlockSpec((tm, GATE_COLS), lambda i: (i, 0))],
        out_shape=[jax.ShapeDtypeStruct((t, n), BF16), jax.ShapeDtypeStruct((t, GATE_COLS), F32)],
        compiler_params=_params(1),
        name="mlstm_inproj",
    )(x, g, w, wg, bg)


def _gated_head_out(h, g_head, sig_o):
    hn = h * lax.rsqrt(jnp.mean(h * h, axis=-1, keepdims=True) + EPS)
    return hn * g_head * sig_o


def _per_row(rep, width):
    return jnp.concatenate([rep] * (width // LANES), axis=1)


def _scan_chunk(proj_ref, gate_ref, gh_ref, hg_ref, c_ref, n_ref, m_ref, r0):
    seq = MLSTM_CHUNK
    d = gh_ref.shape[1]
    dk = d // N_HEADS
    assert seq == LANES
    rows = slice(r0, r0 + seq)

    lane = lax.broadcasted_iota(jnp.int32, (1, LANES), 1)
    li = gate_ref[rows, 0:LANES]
    b = _prefix(_log_sigmoid(gate_ref[rows, LANES:GATE_COLS]), 0, jnp.add, 0.0)
    a = li - b
    m_prev = m_ref[0, 0:1, :]
    neg_mx = -jnp.maximum(m_prev, _prefix(a, 0, jnp.maximum, -jnp.inf))
    exp_neg_meff = jnp.exp(neg_mx - b)
    b_last = b[seq - 1:seq, :]
    g = b_last - b + li
    m_new = jnp.maximum(b_last + m_prev, jnp.max(g, axis=0, keepdims=True))
    decay_row = jnp.exp(b_last + m_prev - m_new)
    wg_t = jnp.exp(g - m_new).T
    a_t = a.T
    m_ref[0, 0:1, :] = jnp.where(lane < N_HEADS, m_new, 0.0)

    causal = (lax.broadcasted_iota(jnp.int32, (seq, seq), 0)
              >= lax.broadcasted_iota(jnp.int32, (seq, seq), 1))
    wg16 = wg_t[0:2 * SUBLANES, :].astype(BF16)
    den = jnp.zeros((seq, LANES), F32)
    heads = range(N_HEADS)
    tok = lambda j, h: proj_ref[rows, j * d + h * dk:j * d + (h + 1) * dk]
    yield
    p, w_rep, qx, upd = [], [], [], []
    for h in heads:
        neg_mx_rep = jnp.broadcast_to(neg_mx[:, h:h + 1], (seq, LANES))
        p.append(jnp.exp(jnp.where(causal, neg_mx_rep + a_t[h:h + 1, :], -jnp.inf)))
        w_rep.append(jnp.exp(neg_mx_rep + m_prev[:, h:h + 1]))
        rhs = jnp.concatenate([tok(1, h), c_ref[0, h].astype(BF16),
                               jnp.broadcast_to(n_ref[0, h:h + 1, :].astype(BF16), (LANES, dk))], axis=0)
        qx.append(lax.dot_general(tok(0, h), rhs, _NT, preferred_element_type=F32))
        vtw = (tok(2, h).T.astype(F32) * wg_t[h:h + 1, :]).astype(BF16)
        upd.append(jnp.dot(jnp.concatenate([vtw, wg16], axis=0), tok(1, h), preferred_element_type=F32))
        yield
    sv, s_sum = [], []
    for h in heads:
        s = qx[h][:, :seq] * p[h]
        s_sum.append(jnp.sum(s, axis=1, keepdims=True))
        sv.append(jnp.dot(s.astype(BF16), tok(2, h), preferred_element_type=F32))
        yield
    nums = []
    for h in heads:
        nums.append(_per_row(w_rep[h], dk) * qx[h][:, seq:seq + dk] + sv[h])
        den = jnp.where(lane == h, w_rep[h] * qx[h][:, seq + dk:] + s_sum[h], den)
        decay = decay_row[:, h:h + 1]
        c_ref[0, h] = decay * c_ref[0, h] + upd[h][:dk]
        n_ref[0, h:h + 1, :] = decay * n_ref[0, h:h + 1, :] + upd[h][dk + h:dk + h + 1, :]
        yield
    recip = 1.0 / jnp.maximum(jnp.abs(den), exp_neg_meff)
    yield
    hh, ssq_rep = [], []
    for h in heads:
        hh.append(nums[h] * _per_row(jnp.broadcast_to(recip[:, h:h + 1], (seq, LANES)), dk))
        ssq_rep.append(jnp.broadcast_to(jnp.sum(hh[h] * hh[h], axis=1, keepdims=True), (seq, LANES)))
        yield
    for h in heads:
        cols = slice(h * dk, (h + 1) * dk)
        hn = hh[h] * _per_row(lax.rsqrt(ssq_rep[h] * (1.0 / dk) + EPS), dk)
        hg_ref[rows, cols] = (hn * gh_ref[:, cols] * tok(3, h).astype(F32)).astype(BF16)
        yield


SCAN_CHUNK_PIECES = 5 * N_HEADS + 2


def _mlstm_long_kernel(*refs, tiles_per_seq, n_cast):
    x_ref, g_ref, w_ref, wg_ref, bg_ref, gh_ref = refs[:6]
    cast_src = refs[6:6 + n_cast]
    hg_ref, c_ref, n_ref, m_ref = refs[6 + n_cast:10 + n_cast]
    cast_dst = refs[10 + n_cast:10 + 2 * n_cast]
    proj_a, gate_a, proj_b, gate_b, w_io, wg_io = refs[10 + 2 * n_cast:]
    i = pl.program_id(0)
    tm = x_ref.shape[0]

    @pl.when(i == 0)
    def _():
        for c0 in range(0, w_ref.shape[0], INPROJ_COLS):
            w_io[:, c0:c0 + INPROJ_COLS] = w_ref[c0:c0 + INPROJ_COLS, :].T
        wg_io[...] = wg_ref[...].T

    @pl.when(lax.rem(jnp.maximum(i - 1, 0), tiles_per_seq) == 0)
    def _():
        c_ref[...] = jnp.zeros_like(c_ref)
        n_ref[...] = jnp.zeros_like(n_ref)
        m_ref[...] = jnp.zeros_like(m_ref)

    n_chunks = tm // MLSTM_CHUNK

    def scan_tile(proj_r, gate_r):
        for c in range(n_chunks):
            yield from _scan_chunk(proj_r, gate_r, gh_ref, hg_ref, c_ref, n_ref, m_ref, c * MLSTM_CHUNK)

    def project_and_cast(proj_w, gate_w):
        casts = list(zip(cast_src, cast_dst))
        for k, _ in enumerate(_inproj_pieces(x_ref, g_ref, w_io, wg_io, bg_ref, proj_w, gate_w,
                                             out_in=False)):
            if casts and k % 3 == 1:
                src, dst = casts.pop()
                dst[...] = src[...].astype(BF16)
            yield
        for src, dst in casts:
            dst[...] = src[...].astype(BF16)

    def step(proj_w, gate_w, proj_r, gate_r):
        _zip_work(scan_tile(proj_r, gate_r), project_and_cast(proj_w, gate_w),
                  n_chunks * SCAN_CHUNK_PIECES, _n_inproj_pieces(w_ref.shape[0]))

    @pl.when(i == 0)
    def _():
        for _ in project_and_cast(proj_a, gate_a):
            pass

    @pl.when(jnp.logical_and(i > 0, lax.rem(i, 2) == 0))
    def _():
        step(proj_a, gate_a, proj_b, gate_b)

    @pl.when(lax.rem(i, 2) == 1)
    def _():
        step(proj_b, gate_b, proj_a, gate_a)


def _mlstm_long(x, g, w, wg, bg, g_head, batch, seq, tm, casts):
    t, d = x.shape
    n = w.shape[0]
    dk = d // N_HEADS
    n_tiles = t // tm
    tiles_per_seq = seq // tm
    assert seq % tm == 0 and tm % MLSTM_CHUNK == 0
    prev = lambda i: jnp.maximum(i - 1, 0)
    cast_in, cast_out, cast_shape = zip(*[_cast_specs(a, n_tiles + 1) for a in casts])
    return pl.pallas_call(
        functools.partial(_mlstm_long_kernel, tiles_per_seq=tiles_per_seq, n_cast=len(casts)),
        grid=(n_tiles + 1,),
        in_specs=[pl.BlockSpec((tm, d), lambda i: (jnp.minimum(i, n_tiles - 1), 0)),
                  _resident((1, d)), _resident((n, d)), _resident((GATE_COLS, d)),
                  _resident((1, GATE_COLS)), _resident((1, d)), *cast_in],
        out_specs=[pl.BlockSpec((tm, d), lambda i: (prev(i), 0)),
                   pl.BlockSpec((1, N_HEADS, dk, dk), lambda i: (prev(i) // tiles_per_seq, 0, 0, 0)),
                   pl.BlockSpec((1, SUBLANES, dk), lambda i: (prev(i) // tiles_per_seq, 0, 0)),
                   pl.BlockSpec((1, SUBLANES, LANES), lambda i: (prev(i) // tiles_per_seq, 0, 0)),
                   *cast_out],
        out_shape=[jax.ShapeDtypeStruct((t, d), BF16),
                   jax.ShapeDtypeStruct((batch, N_HEADS, dk, dk), F32),
                   jax.ShapeDtypeStruct((batch, SUBLANES, dk), F32),
                   jax.ShapeDtypeStruct((batch, SUBLANES, LANES), F32), *cast_shape],
        scratch_shapes=[pltpu.VMEM((tm, n), BF16), pltpu.VMEM((tm, GATE_COLS), F32),
                        pltpu.VMEM((tm, n), BF16), pltpu.VMEM((tm, GATE_COLS), F32),
                        pltpu.VMEM((d, n), BF16), pltpu.VMEM((d, GATE_COLS), BF16)],
        compiler_params=_params(1),
        name="mlstm_long",
    )(x, g, w, wg, bg, g_head, *casts)


SCAN_SAMPLE_GROUP = 4


def _n_scan_sample_pieces(bt):
    return 1 + 4 * (bt * N_HEADS // SCAN_SAMPLE_GROUP)


def _scan_sample_pieces(q_ref, k_ref, v_ref, so_ref, gate_ref, gh_ref, c_in, n_in, m_in,
                        hg_ref, c_out, n_out, m_out, *, seq):
    bt = c_in.shape[0]
    dk = q_ref.shape[1] // N_HEADS
    step = pl.program_id(0)
    row = lax.broadcasted_iota(jnp.int32, (seq, seq), 0)
    col = lax.broadcasted_iota(jnp.int32, (seq, seq), 1)
    causal = row >= col
    diag = row == col
    lane = lax.broadcasted_iota(jnp.int32, (1, LANES), 1)
    to_row = lambda x_col: jnp.sum(jnp.where(diag, x_col, 0.0), axis=0, keepdims=True)

    qs, ks, vs = q_ref[...].astype(F32), k_ref[...].astype(F32), v_ref[...].astype(F32)
    sig_o = so_ref[...].astype(F32)
    pairs = [(j, h) for j in range(bt) for h in range(N_HEADS)]
    tok = lambda x, j, h: x[j * seq:(j + 1) * seq, h * dk:(h + 1) * dk]

    gate_q = {}
    for j in range(bt):
        gates = gate_ref[j * seq:(j + 1) * seq, :]
        b_col = _cumsum(_log_sigmoid(gates), 0)
        m_prev_row = m_in[pl.ds(step * bt + j, 1), :]
        m_new_row = jnp.zeros((1, LANES), F32)
        for h in range(N_HEADS):
            li_c, b_c = gates[:, h:h + 1], b_col[:, LANES + h:LANES + h + 1]
            m_prev = m_prev_row[:, h:h + 1]
            b_last = b_c[seq - 1:seq, :]
            dm = jnp.where(causal, b_c - to_row(b_c) + to_row(li_c), -jnp.inf)
            inter = b_c + m_prev
            m_eff = jnp.maximum(inter, jnp.max(dm, axis=1, keepdims=True))
            g = b_last - b_c + li_c
            m_new = jnp.maximum(b_last + m_prev, jnp.max(g, axis=0, keepdims=True))
            gate_q[j, h] = dict(w_inter=jnp.exp(inter - m_eff), p=jnp.exp(dm - m_eff),
                                floor=jnp.exp(-m_eff), decay=jnp.exp(b_last + m_prev - m_new),
                                wg=jnp.exp(g - m_new))
            m_new_row = jnp.where(lane == h, m_new, m_new_row)
        m_out[pl.ds(step * bt + j, 1), :] = m_new_row
    yield

    def grouped(seq_of_pairs):
        for idx, pair in enumerate(seq_of_pairs, start=1):
            yield pair, idx % SCAN_SAMPLE_GROUP == 0

    pack = 2 * SUBLANES
    assert seq <= pack
    as_tile = lambda a: jnp.concatenate(
        [a, jnp.zeros((pack - a.shape[0], a.shape[1]), F32)], axis=0).astype(BF16)
    qx = {}
    for (j, h), pause in grouped(pairs):
        rhs = jnp.concatenate([c_in[j, h].astype(BF16), as_tile(tok(ks, j, h)),
                               as_tile(n_in[j, h:h + 1, :])], axis=0)
        qx[j, h] = lax.dot_general(as_tile(tok(qs, j, h)), rhs, _NT,
                                   preferred_element_type=F32)[:seq]
        if pause:
            yield

    ones = jnp.ones((seq, LANES), F32)
    sv = {}
    for (j, h), pause in grouped(pairs):
        s = qx[j, h][:, dk:dk + seq] * gate_q[j, h]["p"]
        sv[j, h] = jnp.dot(s, jnp.concatenate([tok(vs, j, h), ones], axis=1),
                           preferred_element_type=F32)
        if pause:
            yield

    for (j, h), pause in grouped(pairs):
        gq = gate_q[j, h]
        num = gq["w_inter"] * qx[j, h][:, :dk] + sv[j, h][:, :dk]
        den = gq["w_inter"] * qx[j, h][:, dk + pack:dk + pack + 1] + sv[j, h][:, dk:dk + 1]
        hh = num * (1.0 / jnp.maximum(jnp.abs(den), gq["floor"]))
        hg_ref[j * seq:(j + 1) * seq, h * dk:(h + 1) * dk] = _gated_head_out(
            hh, gh_ref[:, h * dk:(h + 1) * dk], tok(sig_o, j, h))
        if pause:
            yield

    def update_of(j, h):
        return lax.dot_general(as_tile(tok(vs, j, h) * gate_q[j, h]["wg"]), as_tile(tok(ks, j, h)),
                               _TN, preferred_element_type=F32)

    upd_next = update_of(*pairs[0])
    for idx, ((j, h), pause) in enumerate(grouped(pairs)):
        upd = upd_next
        if idx + 1 < len(pairs):
            upd_next = update_of(*pairs[idx + 1])
        gq = gate_q[j, h]
        c_out[j, h] = gq["decay"] * c_in[j, h] + upd
        n_out[j, h:h + 1, :] = (gq["decay"] * n_in[j, h:h + 1, :]
                                + jnp.sum(tok(ks, j, h) * gq["wg"], axis=0, keepdims=True))
        if pause:
            yield


def _scan_sample_kernel(*refs, seq):
    for _ in _scan_sample_pieces(*refs, seq=seq):
        pass


def _scan_sample_specs(g_head, c0, n0, seq, bt):
    batch = c0.shape[0]
    d = g_head.shape[1]
    dk = d // N_HEADS
    tok = lambda j: pl.BlockSpec((bt * seq, d), lambda i: (i, j))
    c_spec = pl.BlockSpec((bt, N_HEADS, dk, dk), lambda i: (i, 0, 0, 0))
    n_spec = pl.BlockSpec((bt, N_HEADS, dk), lambda i: (i, 0, 0))
    m_spec = pl.BlockSpec((batch, LANES), lambda i: (0, 0))
    in_specs = [tok(0), tok(1), tok(2), tok(3),
                pl.BlockSpec((bt * seq, GATE_COLS), lambda i: (i, 0)),
                _resident((1, d)), c_spec, n_spec, m_spec]
    out_shapes = [jax.ShapeDtypeStruct((batch * seq, d), F32),
                  jax.ShapeDtypeStruct(c0.shape, F32),
                  jax.ShapeDtypeStruct(n0.shape, F32),
                  jax.ShapeDtypeStruct((batch, LANES), F32)]
    return in_specs, [tok(0), c_spec, n_spec, m_spec], out_shapes


def _scan_sample(proj, gates, g_head, c0, n0, m0, seq, bt):
    in_specs, out_specs, out_shapes = _scan_sample_specs(g_head, c0, n0, seq, bt)
    return pl.pallas_call(
        functools.partial(_scan_sample_kernel, seq=seq),
        grid=(c0.shape[0] // bt,),
        in_specs=in_specs, out_specs=out_specs, out_shape=out_shapes,
        compiler_params=_params(1),
        name="mlstm_scan_sample",
    )(proj, proj, proj, proj, gates, g_head, c0, n0, m0)


def _post_pieces(x_ref, z_ref, wo_ref, g_ref, wgu_ref, wd_ref, gfin_ref, out_ref,
                 xn_ref, act_ref, *, final_norm):
    x1 = x_ref[...] + jnp.dot(z_ref[...].astype(BF16), wo_ref[...], preferred_element_type=F32)
    out_ref[...] = x1
    xn_ref[...] = _rmsnorm(x1, g_ref[...]).astype(BF16)
    yield
    ff = wd_ref.shape[0]
    for c in range(ff // FFN_COLS):
        gate = jnp.dot(xn_ref[...], wgu_ref[:, c * FFN_COLS:(c + 1) * FFN_COLS],
                       preferred_element_type=F32)
        up = jnp.dot(xn_ref[...], wgu_ref[:, ff + c * FFN_COLS:ff + (c + 1) * FFN_COLS],
                     preferred_element_type=F32)
        act_ref[:, c * FFN_COLS:(c + 1) * FFN_COLS] = (gate * _sigmoid(gate) * up).astype(BF16)
        yield
    x2 = out_ref[...] + jnp.dot(act_ref[...], wd_ref[...], preferred_element_type=F32)
    if final_norm:
        x2 = _rmsnorm(x2, gfin_ref[...])
    out_ref[...] = x2
    yield


def _n_post_pieces(ff):
    return ff // FFN_COLS + 2


def _post_kernel(*refs, final_norm, scan_seq):
    if scan_seq is None:
        for _ in _post_pieces(*refs, final_norm=final_norm):
            pass
        return
    post_in, scan_in = refs[:7], refs[7:16]
    out_ref, scan_out = refs[16], refs[17:21]
    xn_ref, act_ref = refs[21:]
    _zip_work(_post_pieces(*post_in, out_ref, xn_ref, act_ref, final_norm=final_norm),
              _scan_sample_pieces(*scan_in, *scan_out, seq=scan_seq),
              _n_post_pieces(post_in[5].shape[0]), _n_scan_sample_pieces(scan_in[6].shape[0]))


def _post(x, z, wo, g, wgu, wd, gfin, layer, tm, final_norm, scan=None):
    t, d = x.shape
    ff = wd.shape[1]
    assert ff % FFN_COLS == 0
    n_steps = t // tm
    tile = pl.BlockSpec((tm, d), lambda i: (i, 0))
    of_layer = lambda rows, cols: pl.BlockSpec((None, rows, cols), lambda i: (layer, 0, 0),
                                               pipeline_mode=pl.Buffered(1))
    in_specs = [tile, tile, _resident((d, d)), _resident((1, d)),
                of_layer(d, 2 * ff), of_layer(ff, d), _resident((1, d))]
    out_specs, out_shapes, operands = [tile], [jax.ShapeDtypeStruct((t, d), F32)], [x, z, wo, g, wgu, wd, gfin]
    scan_seq = None
    if scan is not None:
        proj, gates, g_head, c0, n0, m0, scan_seq = scan
        assert c0.shape[0] % n_steps == 0
        s_in, s_out, s_shapes = _scan_sample_specs(g_head, c0, n0, scan_seq, c0.shape[0] // n_steps)
        in_specs, out_specs, out_shapes = in_specs + s_in, out_specs + s_out, out_shapes + s_shapes
        operands += [proj, proj, proj, proj, gates, g_head, c0, n0, m0]
    return pl.pallas_call(
        functools.partial(_post_kernel, final_norm=final_norm, scan_seq=scan_seq),
        grid=(n_steps,),
        in_specs=in_specs, out_specs=out_specs, out_shape=out_shapes,
        scratch_shapes=[pltpu.VMEM((tm, d), BF16), pltpu.VMEM((tm, ff), BF16)],
        compiler_params=_params(1),
        name="outproj_ffn_final" if final_norm else "outproj_ffn",
    )(*operands)


def _conv_proj(x_ref, g_ref, w_ref):
    d = x_ref.shape[1]
    xn = _rmsnorm(x_ref[...], g_ref[...]).astype(BF16)
    part = lambda c: jnp.dot(xn, w_ref[:, c * d:(c + 1) * d], preferred_element_type=F32)
    return part(0), part(1) * part(2)


def _conv_long_pieces(x_ref, g_ref, w_ref, taps_ref, st_out_ref, carry_ref, z_ref, x_copy_ref):
    tm, d = x_ref.shape
    x = x_ref[...]
    x_copy_ref[...] = x
    xn = _rmsnorm(x, g_ref[...]).astype(BF16)
    row = lax.broadcasted_iota(jnp.int32, (tm, 1), 0)
    yield
    for c0 in range(0, d, INPROJ_COLS):
        cols = slice(c0, c0 + INPROJ_COLS)
        part = lambda k: jnp.dot(xn, w_ref[:, k * d + c0:k * d + c0 + INPROJ_COLS],
                                 preferred_element_type=F32)
        bg, u = part(0), part(1) * part(2)
        yield
        s0, s1 = carry_ref[0:1, cols], carry_ref[1:2, cols]
        prev1 = jnp.where(row == 0, s1, pltpu.roll(u, 1, 0))
        prev2 = jnp.where(row == 0, s0, jnp.where(row == 1, s1, pltpu.roll(u, 2, 0)))
        conv = taps_ref[0:1, cols] * prev2 + taps_ref[1:2, cols] * prev1 + taps_ref[2:3, cols] * u
        z_ref[:, cols] = (bg * conv).astype(BF16)
        tail = u[tm - (CONV_W - 1):tm, :]
        carry_ref[:, cols] = tail
        st_out_ref[0, :, cols] = tail
        yield


def _n_conv_long_pieces(d):
    return 1 + 2 * (d // INPROJ_COLS)


def _layer1_long_kernel(x_ref, gmix_ref, wcin_ref, taps_ref, st_ref, wo_ref, gffn_ref, wgu_ref,
                        wd_ref, gfin_ref, out_ref, st_out_ref,
                        carry_ref, z_a, x_a, z_b, x_b, xn_ref, act_ref, *, tiles_per_seq, n_tiles):
    i = pl.program_id(0)

    @pl.when(jnp.logical_and(i < n_tiles, lax.rem(i, tiles_per_seq) == 0))
    def _():
        carry_ref[...] = st_ref[0]

    def conv_tile(z_w, x_w):
        return _conv_long_pieces(x_ref, gmix_ref, wcin_ref, taps_ref, st_out_ref, carry_ref, z_w, x_w)

    def step(z_w, x_w, z_r, x_r):
        _zip_work(_post_pieces(x_r, z_r, wo_ref, gffn_ref, wgu_ref, wd_ref, gfin_ref, out_ref,
                               xn_ref, act_ref, final_norm=True),
                  conv_tile(z_w, x_w),
                  _n_post_pieces(wd_ref.shape[0]), _n_conv_long_pieces(x_ref.shape[1]))

    @pl.when(i == 0)
    def _():
        for _ in conv_tile(z_a, x_a):
            pass

    @pl.when(jnp.logical_and(i > 0, lax.rem(i, 2) == 0))
    def _():
        step(z_a, x_a, z_b, x_b)

    @pl.when(lax.rem(i, 2) == 1)
    def _():
        step(z_b, x_b, z_a, x_a)


def _layer1_long(x, gmix, wcin, taps, state, wo, gffn, wgu, wd, gfin, layer, batch, seq, tm):
    t, d = x.shape
    ff = wd.shape[1]
    keep = CONV_W - 1
    n_tiles = t // tm
    tiles_per_seq = seq // tm
    assert seq % tm == 0
    cur = lambda i: jnp.minimum(i, n_tiles - 1)
    st_spec = pl.BlockSpec((1, keep, d), lambda i: (cur(i) // tiles_per_seq, 0, 0))
    of_layer = lambda rows, cols: pl.BlockSpec((None, rows, cols), lambda i: (layer, 0, 0),
                                               pipeline_mode=pl.Buffered(1))
    return pl.pallas_call(
        functools.partial(_layer1_long_kernel, tiles_per_seq=tiles_per_seq, n_tiles=n_tiles),
        grid=(n_tiles + 1,),
        in_specs=[pl.BlockSpec((tm, d), lambda i: (cur(i), 0)),
                  _resident((1, d)), _resident((d, 3 * d)), _resident((SUBLANES, d)), st_spec,
                  _resident((d, d)), _resident((1, d)), of_layer(d, 2 * ff), of_layer(ff, d),
                  _resident((1, d))],
        out_specs=[pl.BlockSpec((tm, d), lambda i: (jnp.maximum(i - 1, 0), 0)), st_spec],
        out_shape=[jax.ShapeDtypeStruct((t, d), F32), jax.ShapeDtypeStruct((batch, keep, d), F32)],
        scratch_shapes=[pltpu.VMEM((keep, d), F32),
                        pltpu.VMEM((tm, d), BF16), pltpu.VMEM((tm, d), F32),
                        pltpu.VMEM((tm, d), BF16), pltpu.VMEM((tm, d), F32),
                        pltpu.VMEM((tm, d), BF16), pltpu.VMEM((tm, ff), BF16)],
        compiler_params=_params(1),
        name="layer1_long",
    )(x, gmix, wcin, taps, state, wo, gffn, wgu, wd, gfin)


def _conv_short_kernel(x_ref, g_ref, w_ref, taps_ref, st_ref, z_ref, st_out_ref, *, seq):
    tm, d = x_ref.shape
    nseq = tm // seq
    bg, u = _conv_proj(x_ref, g_ref, w_ref)
    pos = lax.broadcasted_iota(jnp.int32, (1, seq, 1), 1)
    s0, s1 = st_ref[:, 0:1, :], st_ref[:, 1:2, :]
    u3 = u.reshape(nseq, seq, d)
    r1 = pltpu.roll(u, 1, 0).reshape(nseq, seq, d)
    r2 = pltpu.roll(u, 2, 0).reshape(nseq, seq, d)
    prev1 = jnp.where(pos == 0, s1, r1)
    prev2 = jnp.where(pos == 0, s0, jnp.where(pos == 1, s1, r2))
    conv = (taps_ref[0:1, :].reshape(1, 1, d) * prev2 + taps_ref[1:2, :].reshape(1, 1, d) * prev1
            + taps_ref[2:3, :].reshape(1, 1, d) * u3)
    z_ref[...] = (bg * conv.reshape(tm, d)).astype(BF16)
    st_out_ref[...] = u3[:, seq - (CONV_W - 1):seq, :]


def _conv_short(x, g, w, taps, state, batch, seq, tm):
    t, d = x.shape
    keep = CONV_W - 1
    assert tm % seq == 0
    out_shape = [jax.ShapeDtypeStruct((t, d), BF16), jax.ShapeDtypeStruct((batch, keep, d), F32)]
    weights = [_resident((1, d)), _resident((d, 3 * d)), _resident((SUBLANES, d))]
    nseq = tm // seq
    st_spec = pl.BlockSpec((nseq, keep, d), lambda i: (i, 0, 0))
    tile = pl.BlockSpec((tm, d), lambda i: (i, 0))
    return pl.pallas_call(
        functools.partial(_conv_short_kernel, seq=seq),
        grid=(t // tm,),
        in_specs=[tile] + weights + [st_spec],
        out_specs=[tile, st_spec],
        out_shape=out_shape,
        compiler_params=_params(1),
        name="gated_conv_short",
    )(x, g, w, taps, state)


def _token_tile(t):
    return min(t, 512)


def _layer1(x1, conv0, wts, batch, seq, tm):
    if seq >= tm:
        y, conv_fin = _layer1_long(x1, wts["norm_mix1"], wts["w_conv_in"], wts["taps"], conv0,
                                   wts["w_conv_out"], wts["norm_ffn1"], wts["w_gu"], wts["w_down"],
                                   wts["norm_final"], 1, batch, seq, tm)
    else:
        z, conv_fin = _conv_short(x1, wts["norm_mix1"], wts["w_conv_in"], wts["taps"], conv0,
                                  batch, seq, tm)
        y, = _post(x1, z, wts["w_conv_out"], wts["norm_ffn1"], wts["w_gu"], wts["w_down"],
                   wts["norm_final"], 1, tm, final_norm=True)
    return y.reshape(batch, seq, -1), conv_fin[None]


def kernel(x_prompt, x_sample, state_mlstm_C, state_mlstm_n, state_mlstm_m, state_conv, norm_mix, norm_ffn, norm_final, w_mlstm_in, b_mlstm_gate, mlstm_head_norm, w_mlstm_out, w_conv_in, w_conv_taps, w_conv_out, w_ffn_gate_up, w_ffn_down):
    d = x_prompt.shape[-1]
    dk = d // N_HEADS
    assert norm_mix.shape[0] == 2 and w_mlstm_in.shape[0] == 1 and w_conv_in.shape[0] == 1
    assert w_conv_taps.shape[1] == CONV_W and 2 * N_HEADS <= LANES
    row = lambda v: v.reshape(1, -1).astype(F32)
    pad_cols = lambda m: jnp.pad(m, ((0, 0), (0, LANES - N_HEADS)))
    pad_rows = lambda m: jnp.pad(m, ((0, LANES - N_HEADS), (0, 0)))
    w_in_t = jnp.swapaxes(w_mlstm_in[0], 0, 1)
    q_scale = jnp.concatenate([jnp.full((d, 1), dk ** -0.5, F32), jnp.ones((3 * d, 1), F32)])
    wts = {
        "norm_mix0": row(norm_mix[0]), "norm_mix1": row(norm_mix[1]),
        "norm_ffn0": row(norm_ffn[0]), "norm_ffn1": row(norm_ffn[1]),
        "norm_final": row(norm_final), "g_head": row(mlstm_head_norm[0]),
        "w_in": (w_in_t[:4 * d] * q_scale).astype(BF16),
        "w_gate": jnp.concatenate([pad_rows(w_in_t[4 * d:4 * d + N_HEADS]),
                                   pad_rows(w_in_t[4 * d + N_HEADS:])], axis=0).astype(BF16),
        "b_gate": jnp.concatenate([pad_cols(b_mlstm_gate[:, :N_HEADS]),
                                   pad_cols(b_mlstm_gate[:, N_HEADS:])], axis=1).astype(F32),
        "taps": jnp.pad(w_conv_taps[0], ((0, SUBLANES - CONV_W), (0, 0))).astype(F32),
    }
    bp, sp, _ = x_prompt.shape
    bs, ss, _ = x_sample.shape
    tp, ts = bp * sp, bs * ss
    tm_p, tm_s = _token_tile(tp), _token_tile(ts)
    xp, xs = x_prompt.reshape(tp, d), x_sample.reshape(ts, d)
    n_layers, _, two_ff = w_ffn_gate_up.shape
    ff = two_ff // 2

    hg_p, c_p, n_pad, m_pad, w_gu, w_down, wts["w_conv_in"], wts["w_conv_out"], w_out0 = _mlstm_long(
        xp, wts["norm_mix0"], wts["w_in"], wts["w_gate"], wts["b_gate"], wts["g_head"], bp, sp, tm_p,
        casts=[w_ffn_gate_up.reshape(n_layers * d, two_ff), w_ffn_down.reshape(n_layers * ff, d),
               w_conv_in[0], w_conv_out[0], w_mlstm_out[0]])
    wts["w_gu"], wts["w_down"] = w_gu.reshape(n_layers, d, two_ff), w_down.reshape(n_layers, ff, d)
    n_p, m_p = n_pad[:, :N_HEADS, :], m_pad[:, 0, :N_HEADS]

    proj_s, gates_s = _inproj(xs, wts["norm_mix0"], wts["w_in"], wts["w_gate"], wts["b_gate"], tm_s)
    m0_pad = jnp.pad(state_mlstm_m[0], ((0, 0), (0, LANES - N_HEADS)))
    scan_args = (proj_s, gates_s, wts["g_head"], state_mlstm_C[0], state_mlstm_n[0], m0_pad)
    post0 = functools.partial(_post, wo=w_out0, g=wts["norm_ffn0"], wgu=wts["w_gu"], wd=wts["w_down"],
                              gfin=wts["norm_final"], layer=0, final_norm=False)
    if bs % (tp // tm_p) == 0:
        x1_p, hg_s, c_s, n_s, m_s_pad = post0(xp, hg_p, tm=tm_p, scan=scan_args + (ss,))
    else:
        x1_p, = post0(xp, hg_p, tm=tm_p)
        hg_s, c_s, n_s, m_s_pad = _scan_sample(*scan_args, ss, 4)
    x1_s, = post0(xs, hg_s, tm=tm_s)

    y_p, conv_p = _layer1(x1_p, jnp.zeros((bp, CONV_W - 1, d), F32), wts, bp, sp, tm_p)
    y_s, conv_s = _layer1(x1_s, state_conv[0], wts, bs, ss, tm_s)
    return (y_p, y_s, c_p[None], n_p[None], m_p[None], conv_p,
            c_s[None], n_s[None], m_s_pad[:, :N_HEADS][None], conv_s)
```

```python
import functools

import jax
import jax.numpy as jnp
from jax import lax
from jax.experimental import pallas as pl
from jax.experimental.pallas import tpu as pltpu

EPS = 1e-6
N_HEADS = 4
MLSTM_CHUNK = 128
CONV_W = 3
LANES = 128
SUBLANES = 8
GATE_COLS = 2 * LANES
FFN_COLS = 256
VMEM_LIMIT = 56 * 1024 * 1024

F32 = jnp.float32
BF16 = jnp.bfloat16
_NT = (((1,), (1,)), ((), ()))
_TN = (((0,), (0,)), ((), ()))


def _params(n_grid):
    return pltpu.CompilerParams(
        dimension_semantics=("arbitrary",) * n_grid, vmem_limit_bytes=VMEM_LIMIT)


def _resident(shape):
    zeros = (0,) * len(shape)
    return pl.BlockSpec(shape, lambda *_: zeros, pipeline_mode=pl.Buffered(1))


def _rmsnorm(x, g):
    return x * lax.rsqrt(jnp.mean(x * x, axis=-1, keepdims=True) + EPS) * g


def _sigmoid(x):
    return 1.0 / (1.0 + jnp.exp(-x))


def _log_sigmoid(x):
    return jnp.minimum(x, 0.0) - jnp.log1p(jnp.exp(-jnp.abs(x)))


def _prefix(x, axis, combine, identity):
    n = x.shape[axis]
    idx = lax.broadcasted_iota(jnp.int32, x.shape, axis)
    shift = 1
    while shift < n:
        x = combine(x, jnp.where(idx >= shift, pltpu.roll(x, shift, axis), identity))
        shift *= 2
    return x


def _cumsum(x, axis):
    return _prefix(x, axis, jnp.add, 0.0)


INPROJ_COLS = 256


def _inproj_pieces(x_ref, g_ref, w_ref, wg_ref, bg_ref, proj_ref, gate_ref, *, out_in):
    d = x_ref.shape[1]
    n = w_ref.shape[0] if out_in else w_ref.shape[1]
    if out_in:
        times = lambda w, cols: lax.dot_general(xn, w[cols, :], _NT, preferred_element_type=F32)
    else:
        times = lambda w, cols: jnp.dot(xn, w[:, cols], preferred_element_type=F32)
    xn = _rmsnorm(x_ref[...], g_ref[...]).astype(BF16)
    yield
    for c0 in range(0, n, INPROJ_COLS):
        cols = slice(c0, c0 + INPROJ_COLS)
        y = times(w_ref, cols)
        if c0 >= n - d:
            y = _sigmoid(y)
        proj_ref[:, cols] = y.astype(BF16)
        yield
    gate_ref[...] = times(wg_ref, slice(0, GATE_COLS)) + bg_ref[...]
    yield


def _n_inproj_pieces(n):
    return n // INPROJ_COLS + 2


def _inproj_kernel(*refs):
    for _ in _inproj_pieces(*refs, out_in=True):
        pass


def _zip_work(primary, secondary, n_primary, n_secondary):
    next(secondary, None)
    done = 1
    for k, _ in enumerate(primary, start=1):
        while done * n_primary < k * n_secondary:
            next(secondary, None)
            done += 1
    for _ in secondary:
        pass


def _cast_specs(arr, n_steps):
    rows, cols = arr.shape
    pack = 2 * SUBLANES
    nb = max(n for n in range(1, n_steps + 1) if rows % n == 0 and (rows // n) % pack == 0)
    blk = lambda i: (jnp.minimum(i * nb // n_steps, nb - 1), 0)
    spec = pl.BlockSpec((rows // nb, cols), blk)
    return spec, spec, jax.ShapeDtypeStruct((rows, cols), BF16)


def _inproj(x, g, w, wg, bg, tm):
    t, d = x.shape
    n = 4 * d
    return pl.pallas_call(
        _inproj_kernel,
        grid=(t // tm,),
        in_specs=[pl.BlockSpec((tm, d), lambda i: (i, 0)),
                  _resident((1, d)), _resident((n, d)), _resident((GATE_COLS, d)), _resident((1, GATE_COLS))],
        out_specs=[pl.BlockSpec((tm, n), lambda i: (i, 0)),
                   pl.BlockSpec((tm, GATE_COLS), lambda i: (i, 0))],
        out_shape=[jax.ShapeDtypeStruct((t, n), BF16), jax.ShapeDtypeStruct((t, GATE_COLS), F32)],
        compiler_params=_params(1),
        name="mlstm_inproj",
    )(x, g, w, wg, bg)


def _gated_head_out(h, g_head, sig_o):
    hn = h * lax.rsqrt(jnp.mean(h * h, axis=-1, keepdims=True) + EPS)
    return hn * g_head * sig_o


def _per_row(rep, width):
    return jnp.concatenate([rep] * (width // LANES), axis=1)


def _scan_chunk(proj_ref, gate_ref, gh_ref, hg_ref, c_ref, n_ref, m_ref, r0):
    seq = MLSTM_CHUNK
    d = gh_ref.shape[1]
    dk = d // N_HEADS
    assert seq == LANES
    rows = slice(r0, r0 + seq)

    lane = lax.broadcasted_iota(jnp.int32, (1, LANES), 1)
    li = gate_ref[rows, 0:LANES]
    b = _prefix(_log_sigmoid(gate_ref[rows, LANES:GATE_COLS]), 0, jnp.add, 0.0)
    a = li - b
    m_prev = m_ref[0, 0:1, :]
    neg_mx = -jnp.maximum(m_prev, _prefix(a, 0, jnp.maximum, -jnp.inf))
    exp_neg_meff = jnp.exp(neg_mx - b)
    b_last = b[seq - 1:seq, :]
    g = b_last - b + li
    m_new = jnp.maximum(b_last + m_prev, jnp.max(g, axis=0, keepdims=True))
    decay_row = jnp.exp(b_last + m_prev - m_new)
    wg_t = jnp.exp(g - m_new).T
    a_t = a.T
    m_ref[0, 0:1, :] = jnp.where(lane < N_HEADS, m_new, 0.0)

    causal = (lax.broadcasted_iota(jnp.int32, (seq, seq), 0)
              >= lax.broadcasted_iota(jnp.int32, (seq, seq), 1))
    wg16 = wg_t[0:2 * SUBLANES, :].astype(BF16)
    den = jnp.zeros((seq, LANES), F32)
    heads = range(N_HEADS)
    tok = lambda j, h: proj_ref[rows, j * d + h * dk:j * d + (h + 1) * dk]
    yield
    p, w_rep, qx, upd = [], [], [], []
    for h in heads:
        neg_mx_rep = jnp.broadcast_to(neg_mx[:, h:h + 1], (seq, LANES))
        p.append(jnp.exp(jnp.where(causal, neg_mx_rep + a_t[h:h + 1, :], -jnp.inf)))
        w_rep.append(jnp.exp(neg_mx_rep + m_prev[:, h:h + 1]))
        rhs = jnp.concatenate([tok(1, h), c_ref[0, h].astype(BF16),
                               jnp.broadcast_to(n_ref[0, h:h + 1, :].astype(BF16), (LANES, dk))], axis=0)
        qx.append(lax.dot_general(tok(0, h), rhs, _NT, preferred_element_type=F32))
        vtw = (tok(2, h).T.astype(F32) * wg_t[h:h + 1, :]).astype(BF16)
        upd.append(jnp.dot(jnp.concatenate([vtw, wg16], axis=0), tok(1, h), preferred_element_type=F32))
        yield
    sv, s_sum = [], []
    for h in heads:
        s = qx[h][:, :seq] * p[h]
        s_sum.append(jnp.sum(s, axis=1, keepdims=True))
        sv.append(jnp.dot(s.astype(BF16), tok(2, h), preferred_element_type=F32))
        yield
    nums = []
    for h in heads:
        nums.append(_per_row(w_rep[h], dk) * qx[h][:, seq:seq + dk] + sv[h])
        den = jnp.where(lane == h, w_rep[h] * qx[h][:, seq + dk:] + s_sum[h], den)
        decay = decay_row[:, h:h + 1]
        c_ref[0, h] = decay * c_ref[0, h] + upd[h][:dk]
        n_ref[0, h:h + 1, :] = decay * n_ref[0, h:h + 1, :] + upd[h][dk + h:dk + h + 1, :]
        yield
    recip = 1.0 / jnp.maximum(jnp.abs(den), exp_neg_meff)
    yield
    hh, ssq_rep = [], []
    for h in heads:
        hh.append(nums[h] * _per_row(jnp.broadcast_to(recip[:, h:h + 1], (seq, LANES)), dk))
        ssq_rep.append(jnp.broadcast_to(jnp.sum(hh[h] * hh[h], axis=1, keepdims=True), (seq, LANES)))
        yield
    for h in heads:
        cols = slice(h * dk, (h + 1) * dk)
        hn = hh[h] * _per_row(lax.rsqrt(ssq_rep[h] * (1.0 / dk) + EPS), dk)
        hg_ref[rows, cols] = (hn * gh_ref[:, cols] * tok(3, h).astype(F32)).astype(BF16)
        yield


SCAN_CHUNK_PIECES = 5 * N_HEADS + 2


def _mlstm_long_kernel(*refs, tiles_per_seq, n_cast):
    x_ref, g_ref, w_ref, wg_ref, bg_ref, gh_ref = refs[:6]
    cast_src = refs[6:6 + n_cast]
    hg_ref, c_ref, n_ref, m_ref = refs[6 + n_cast:10 + n_cast]
    cast_dst = refs[10 + n_cast:10 + 2 * n_cast]
    proj_a, gate_a, proj_b, gate_b, w_io, wg_io = refs[10 + 2 * n_cast:]
    i = pl.program_id(0)
    tm = x_ref.shape[0]

    @pl.when(i == 0)
    def _():
        for c0 in range(0, w_ref.shape[0], INPROJ_COLS):
            w_io[:, c0:c0 + INPROJ_COLS] = w_ref[c0:c0 + INPROJ_COLS, :].T
        wg_io[...] = wg_ref[...].T

    @pl.when(lax.rem(jnp.maximum(i - 1, 0), tiles_per_seq) == 0)
    def _():
        c_ref[...] = jnp.zeros_like(c_ref)
        n_ref[...] = jnp.zeros_like(n_ref)
        m_ref[...] = jnp.zeros_like(m_ref)

    n_chunks = tm // MLSTM_CHUNK

    def scan_tile(proj_r, gate_r):
        for c in range(n_chunks):
            yield from _scan_chunk(proj_r, gate_r, gh_ref, hg_ref, c_ref, n_ref, m_ref, c * MLSTM_CHUNK)

    def project_and_cast(proj_w, gate_w):
        casts = list(zip(cast_src, cast_dst))
        for k, _ in enumerate(_inproj_pieces(x_ref, g_ref, w_io, wg_io, bg_ref, proj_w, gate_w,
                                             out_in=False)):
            if casts and k % 3 == 1:
                src, dst = casts.pop()
                dst[...] = src[...].astype(BF16)
            yield
        for src, dst in casts:
            dst[...] = src[...].astype(BF16)

    def step(proj_w, gate_w, proj_r, gate_r):
        _zip_work(scan_tile(proj_r, gate_r), project_and_cast(proj_w, gate_w),
                  n_chunks * SCAN_CHUNK_PIECES, _n_inproj_pieces(w_ref.shape[0]))

    @pl.when(i == 0)
    def _():
        for _ in project_and_cast(proj_a, gate_a):
            pass

    @pl.when(jnp.logical_and(i > 0, lax.rem(i, 2) == 0))
    def _():
        step(proj_a, gate_a, proj_b, gate_b)

    @pl.when(lax.rem(i, 2) == 1)
    def _():
        step(proj_b, gate_b, proj_a, gate_a)


def _mlstm_long(x, g, w, wg, bg, g_head, batch, seq, tm, casts):
    t, d = x.shape
    n = 4 * d
    dk = d // N_HEADS
    n_tiles = t // tm
    tiles_per_seq = seq // tm
    assert seq % tm == 0 and tm % MLSTM_CHUNK == 0
    prev = lambda i: jnp.maximum(i - 1, 0)
    cast_in, cast_out, cast_shape = zip(*[_cast_specs(a, n_tiles + 1) for a in casts])
    return pl.pallas_call(
        functools.partial(_mlstm_long_kernel, tiles_per_seq=tiles_per_seq, n_cast=len(casts)),
        grid=(n_tiles + 1,),
        in_specs=[pl.BlockSpec((tm, d), lambda i: (jnp.minimum(i, n_tiles - 1), 0)),
                  _resident((1, d)), _resident((n, d)), _resident((GATE_COLS, d)),
                  _resident((1, GATE_COLS)), _resident((1, d)), *cast_in],
        out_specs=[pl.BlockSpec((tm, d), lambda i: (prev(i), 0)),
                   pl.BlockSpec((1, N_HEADS, dk, dk), lambda i: (prev(i) // tiles_per_seq, 0, 0, 0)),
                   pl.BlockSpec((1, SUBLANES, dk), lambda i: (prev(i) // tiles_per_seq, 0, 0)),
                   pl.BlockSpec((1, SUBLANES, LANES), lambda i: (prev(i) // tiles_per_seq, 0, 0)),
                   *cast_out],
        out_shape=[jax.ShapeDtypeStruct((t, d), BF16),
                   jax.ShapeDtypeStruct((batch, N_HEADS, dk, dk), F32),
                   jax.ShapeDtypeStruct((batch, SUBLANES, dk), F32),
                   jax.ShapeDtypeStruct((batch, SUBLANES, LANES), F32), *cast_shape],
        scratch_shapes=[pltpu.VMEM((tm, n), BF16), pltpu.VMEM((tm, GATE_COLS), F32),
                        pltpu.VMEM((tm, n), BF16), pltpu.VMEM((tm, GATE_COLS), F32),
                        pltpu.VMEM((d, n), BF16), pltpu.VMEM((d, GATE_COLS), BF16)],
        compiler_params=_params(1),
        name="mlstm_long",
    )(x, g, w, wg, bg, g_head, *casts)


SCAN_SAMPLE_GROUP = 4


def _n_scan_sample_pieces(bt):
    return 1 + 4 * (bt * N_HEADS // SCAN_SAMPLE_GROUP)


def _scan_sample_pieces(q_ref, k_ref, v_ref, so_ref, gate_ref, gh_ref, c_in, n_in, m_in,
                        hg_ref, c_out, n_out, m_out, *, seq):
    bt = c_in.shape[0]
    dk = q_ref.shape[1] // N_HEADS
    step = pl.program_id(0)
    row = lax.broadcasted_iota(jnp.int32, (seq, seq), 0)
    col = lax.broadcasted_iota(jnp.int32, (seq, seq), 1)
    causal = row >= col
    diag = row == col
    lane = lax.broadcasted_iota(jnp.int32, (1, LANES), 1)
    to_row = lambda x_col: jnp.sum(jnp.where(diag, x_col, 0.0), axis=0, keepdims=True)

    qs, ks, vs = q_ref[...].astype(F32), k_ref[...].astype(F32), v_ref[...].astype(F32)
    sig_o = so_ref[...].astype(F32)
    pairs = [(j, h) for j in range(bt) for h in range(N_HEADS)]
    tok = lambda x, j, h: x[j * seq:(j + 1) * seq, h * dk:(h + 1) * dk]

    gate_q = {}
    for j in range(bt):
        gates = gate_ref[j * seq:(j + 1) * seq, :]
        b_col = _cumsum(_log_sigmoid(gates), 0)
        m_prev_row = m_in[pl.ds(step * bt + j, 1), :]
        m_new_row = jnp.zeros((1, LANES), F32)
        for h in range(N_HEADS):
            li_c, b_c = gates[:, h:h + 1], b_col[:, LANES + h:LANES + h + 1]
            m_prev = m_prev_row[:, h:h + 1]
            b_last = b_c[seq - 1:seq, :]
            dm = jnp.where(causal, b_c - to_row(b_c) + to_row(li_c), -jnp.inf)
            inter = b_c + m_prev
            m_eff = jnp.maximum(inter, jnp.max(dm, axis=1, keepdims=True))
            g = b_last - b_c + li_c
            m_new = jnp.maximum(b_last + m_prev, jnp.max(g, axis=0, keepdims=True))
            gate_q[j, h] = dict(w_inter=jnp.exp(inter - m_eff), p=jnp.exp(dm - m_eff),
                                floor=jnp.exp(-m_eff), decay=jnp.exp(b_last + m_prev - m_new),
                                wg=jnp.exp(g - m_new))
            m_new_row = jnp.where(lane == h, m_new, m_new_row)
        m_out[pl.ds(step * bt + j, 1), :] = m_new_row
    yield

    def grouped(seq_of_pairs):
        for idx, pair in enumerate(seq_of_pairs, start=1):
            yield pair, idx % SCAN_SAMPLE_GROUP == 0

    pack = 2 * SUBLANES
    assert seq <= pack
    as_tile = lambda a: jnp.concatenate(
        [a, jnp.zeros((pack - a.shape[0], a.shape[1]), F32)], axis=0).astype(BF16)
    qx = {}
    for (j, h), pause in grouped(pairs):
        rhs = jnp.concatenate([c_in[j, h].astype(BF16), as_tile(tok(ks, j, h)),
                               as_tile(n_in[j, h:h + 1, :])], axis=0)
        qx[j, h] = lax.dot_general(as_tile(tok(qs, j, h)), rhs, _NT,
                                   preferred_element_type=F32)[:seq]
        if pause:
            yield

    sv, s_sum = {}, {}
    for (j, h), pause in grouped(pairs):
        s = qx[j, h][:, dk:dk + seq] * gate_q[j, h]["p"]
        v = tok(vs, j, h)
        acc = s[:, 0:1] * v[0:1, :]
        for t in range(1, seq):
            acc = acc + s[:, t:t + 1] * v[t:t + 1, :]
        sv[j, h], s_sum[j, h] = acc, jnp.sum(s, axis=1, keepdims=True)
        if pause:
            yield

    for (j, h), pause in grouped(pairs):
        gq = gate_q[j, h]
        num = gq["w_inter"] * qx[j, h][:, :dk] + sv[j, h]
        den = gq["w_inter"] * qx[j, h][:, dk + pack:dk + pack + 1] + s_sum[j, h]
        hh = num * (1.0 / jnp.maximum(jnp.abs(den), gq["floor"]))
        hg_ref[j * seq:(j + 1) * seq, h * dk:(h + 1) * dk] = _gated_head_out(
            hh, gh_ref[:, h * dk:(h + 1) * dk], tok(sig_o, j, h))
        if pause:
            yield

    def update_of(j, h):
        return lax.dot_general(as_tile(tok(vs, j, h) * gate_q[j, h]["wg"]), as_tile(tok(ks, j, h)),
                               _TN, preferred_element_type=F32)

    upd_next = update_of(*pairs[0])
    for idx, ((j, h), pause) in enumerate(grouped(pairs)):
        upd = upd_next
        if idx + 1 < len(pairs):
            upd_next = update_of(*pairs[idx + 1])
        gq = gate_q[j, h]
        c_out[j, h] = gq["decay"] * c_in[j, h] + upd
        n_out[j, h:h + 1, :] = (gq["decay"] * n_in[j, h:h + 1, :]
                                + jnp.sum(tok(ks, j, h) * gq["wg"], axis=0, keepdims=True))
        if pause:
            yield


def _scan_sample_kernel(*refs, seq):
    for _ in _scan_sample_pieces(*refs, seq=seq):
        pass


def _scan_sample_specs(g_head, c0, n0, seq, bt):
    batch = c0.shape[0]
    d = g_head.shape[1]
    dk = d // N_HEADS
    tok = lambda j: pl.BlockSpec((bt * seq, d), lambda i: (i, j))
    c_spec = pl.BlockSpec((bt, N_HEADS, dk, dk), lambda i: (i, 0, 0, 0))
    n_spec = pl.BlockSpec((bt, N_HEADS, dk), lambda i: (i, 0, 0))
    m_spec = pl.BlockSpec((batch, LANES), lambda i: (0, 0))
    in_specs = [tok(0), tok(1), tok(2), tok(3),
                pl.BlockSpec((bt * seq, GATE_COLS), lambda i: (i, 0)),
                _resident((1, d)), c_spec, n_spec, m_spec]
    out_shapes = [jax.ShapeDtypeStruct((batch * seq, d), F32),
                  jax.ShapeDtypeStruct(c0.shape, F32),
                  jax.ShapeDtypeStruct(n0.shape, F32),
                  jax.ShapeDtypeStruct((batch, LANES), F32)]
    return in_specs, [tok(0), c_spec, n_spec, m_spec], out_shapes


def _scan_sample(proj, gates, g_head, c0, n0, m0, seq, bt):
    in_specs, out_specs, out_shapes = _scan_sample_specs(g_head, c0, n0, seq, bt)
    return pl.pallas_call(
        functools.partial(_scan_sample_kernel, seq=seq),
        grid=(c0.shape[0] // bt,),
        in_specs=in_specs, out_specs=out_specs, out_shape=out_shapes,
        compiler_params=_params(1),
        name="mlstm_scan_sample",
    )(proj, proj, proj, proj, gates, g_head, c0, n0, m0)


def _post_pieces(x_ref, z_ref, wo_ref, g_ref, wgu_ref, wd_ref, gfin_ref, out_ref,
                 xn_ref, act_ref, *, final_norm):
    x1 = x_ref[...] + jnp.dot(z_ref[...].astype(BF16), wo_ref[...], preferred_element_type=F32)
    out_ref[...] = x1
    xn_ref[...] = _rmsnorm(x1, g_ref[...]).astype(BF16)
    yield
    ff = wd_ref.shape[0]
    for c in range(ff // FFN_COLS):
        gate = jnp.dot(xn_ref[...], wgu_ref[:, c * FFN_COLS:(c + 1) * FFN_COLS],
                       preferred_element_type=F32)
        up = jnp.dot(xn_ref[...], wgu_ref[:, ff + c * FFN_COLS:ff + (c + 1) * FFN_COLS],
                     preferred_element_type=F32)
        act_ref[:, c * FFN_COLS:(c + 1) * FFN_COLS] = (gate * _sigmoid(gate) * up).astype(BF16)
        yield
    x2 = out_ref[...] + jnp.dot(act_ref[...], wd_ref[...], preferred_element_type=F32)
    if final_norm:
        x2 = _rmsnorm(x2, gfin_ref[...])
    out_ref[...] = x2
    yield


def _n_post_pieces(ff):
    return ff // FFN_COLS + 2


def _post_kernel(*refs, scan_seq):
    if scan_seq is None:
        for _ in _post_pieces(*refs, final_norm=False):
            pass
        return
    post_in, scan_in = refs[:7], refs[7:16]
    out_ref, scan_out = refs[16], refs[17:21]
    xn_ref, act_ref = refs[21:]
    _zip_work(_post_pieces(*post_in, out_ref, xn_ref, act_ref, final_norm=False),
              _scan_sample_pieces(*scan_in, *scan_out, seq=scan_seq),
              _n_post_pieces(post_in[5].shape[0]), _n_scan_sample_pieces(scan_in[6].shape[0]))


def _post(x, z, wo, g, wgu, wd, gfin, layer, tm, scan=None):
    t, d = x.shape
    ff = wd.shape[1]
    assert ff % FFN_COLS == 0
    n_steps = t // tm
    tile = pl.BlockSpec((tm, d), lambda i: (i, 0))
    of_layer = lambda rows, cols: pl.BlockSpec((None, rows, cols), lambda i: (layer, 0, 0),
                                               pipeline_mode=pl.Buffered(1))
    in_specs = [tile, tile, _resident((d, d)), _resident((1, d)),
                of_layer(d, 2 * ff), of_layer(ff, d), _resident((1, d))]
    out_specs, out_shapes, operands = [tile], [jax.ShapeDtypeStruct((t, d), F32)], [x, z, wo, g, wgu, wd, gfin]
    scan_seq = None
    if scan is not None:
        proj, gates, g_head, c0, n0, m0, scan_seq = scan
        assert c0.shape[0] % n_steps == 0
        s_in, s_out, s_shapes = _scan_sample_specs(g_head, c0, n0, scan_seq, c0.shape[0] // n_steps)
        in_specs, out_specs, out_shapes = in_specs + s_in, out_specs + s_out, out_shapes + s_shapes
        operands += [proj, proj, proj, proj, gates, g_head, c0, n0, m0]
    return pl.pallas_call(
        functools.partial(_post_kernel, scan_seq=scan_seq),
        grid=(n_steps,),
        in_specs=in_specs, out_specs=out_specs, out_shape=out_shapes,
        scratch_shapes=[pltpu.VMEM((tm, d), BF16), pltpu.VMEM((tm, ff), BF16)],
        compiler_params=_params(1),
        name="outproj_ffn",
    )(*operands)


def _conv_proj(x_ref, g_ref, w_ref):
    d = x_ref.shape[1]
    xn = _rmsnorm(x_ref[...], g_ref[...]).astype(BF16)
    part = lambda c: jnp.dot(xn, w_ref[:, c * d:(c + 1) * d], preferred_element_type=F32)
    return part(0), part(1) * part(2)


def _conv_long_pieces(x_ref, g_ref, w_ref, taps_ref, st_out_ref, carry_ref, z_ref, x_copy_ref):
    tm, d = x_ref.shape
    x = x_ref[...]
    x_copy_ref[...] = x
    xn = _rmsnorm(x, g_ref[...]).astype(BF16)
    row = lax.broadcasted_iota(jnp.int32, (tm, 1), 0)
    yield
    for c0 in range(0, d, INPROJ_COLS):
        cols = slice(c0, c0 + INPROJ_COLS)
        part = lambda k: jnp.dot(xn, w_ref[:, k * d + c0:k * d + c0 + INPROJ_COLS],
                                 preferred_element_type=F32)
        bg, u = part(0), part(1) * part(2)
        yield
        s0, s1 = carry_ref[0:1, cols], carry_ref[1:2, cols]
        prev1 = jnp.where(row == 0, s1, pltpu.roll(u, 1, 0))
        prev2 = jnp.where(row == 0, s0, jnp.where(row == 1, s1, pltpu.roll(u, 2, 0)))
        conv = taps_ref[0:1, cols] * prev2 + taps_ref[1:2, cols] * prev1 + taps_ref[2:3, cols] * u
        z_ref[:, cols] = (bg * conv).astype(BF16)
        tail = u[tm - (CONV_W - 1):tm, :]
        carry_ref[:, cols] = tail
        st_out_ref[0, :, cols] = tail
        yield


def _n_conv_long_pieces(d):
    return 1 + 2 * (d // INPROJ_COLS)


def _layer1_long_kernel(x_ref, gmix_ref, wcin_ref, taps_ref, st_ref, wo_ref, gffn_ref, wgu_ref,
                        wd_ref, gfin_ref, out_ref, st_out_ref,
                        carry_ref, z_a, x_a, z_b, x_b, xn_ref, act_ref, *, tiles_per_seq, n_tiles):
    i = pl.program_id(0)

    @pl.when(jnp.logical_and(i < n_tiles, lax.rem(i, tiles_per_seq) == 0))
    def _():
        carry_ref[...] = st_ref[0]

    def conv_tile(z_w, x_w):
        return _conv_long_pieces(x_ref, gmix_ref, wcin_ref, taps_ref, st_out_ref, carry_ref, z_w, x_w)

    def post_tile(z_r, x_r):
        return _post_pieces(x_r, z_r, wo_ref, gffn_ref, wgu_ref, wd_ref, gfin_ref, out_ref,
                            xn_ref, act_ref, final_norm=True)

    def step(z_w, x_w, z_r, x_r):
        _zip_work(post_tile(z_r, x_r), conv_tile(z_w, x_w),
                  _n_post_pieces(wd_ref.shape[0]), _n_conv_long_pieces(x_ref.shape[1]))

    @pl.when(i == 0)
    def _():
        for _ in conv_tile(z_a, x_a):
            pass

    @pl.when(jnp.logical_and(i > 0, lax.rem(i, 2) == 0))
    def _():
        step(z_a, x_a, z_b, x_b)

    @pl.when(lax.rem(i, 2) == 1)
    def _():
        step(z_b, x_b, z_a, x_a)


def _layer1_long(x, gmix, wcin, taps, state, wo, gffn, wgu, wd, gfin, layer, batch, seq, tm):
    t, d = x.shape
    ff = wd.shape[1]
    keep = CONV_W - 1
    n_tiles = t // tm
    tiles_per_seq = seq // tm
    assert seq % tm == 0
    cur = lambda i: jnp.minimum(i, n_tiles - 1)
    st_spec = pl.BlockSpec((1, keep, d), lambda i: (cur(i) // tiles_per_seq, 0, 0))
    of_layer = lambda rows, cols: pl.BlockSpec((None, rows, cols), lambda i: (layer, 0, 0),
                                               pipeline_mode=pl.Buffered(1))
    return pl.pallas_call(
        functools.partial(_layer1_long_kernel, tiles_per_seq=tiles_per_seq, n_tiles=n_tiles),
        grid=(n_tiles + 1,),
        in_specs=[pl.BlockSpec((tm, d), lambda i: (cur(i), 0)),
                  _resident((1, d)), _resident((d, 3 * d)), _resident((SUBLANES, d)), st_spec,
                  _resident((d, d)), _resident((1, d)), of_layer(d, 2 * ff), of_layer(ff, d),
                  _resident((1, d))],
        out_specs=[pl.BlockSpec((tm, d), lambda i: (jnp.maximum(i - 1, 0), 0)), st_spec],
        out_shape=[jax.ShapeDtypeStruct((t, d), F32), jax.ShapeDtypeStruct((batch, keep, d), F32)],
        scratch_shapes=[pltpu.VMEM((keep, d), F32),
                        pltpu.VMEM((tm, d), BF16), pltpu.VMEM((tm, d), F32),
                        pltpu.VMEM((tm, d), BF16), pltpu.VMEM((tm, d), F32),
                        pltpu.VMEM((tm, d), BF16), pltpu.VMEM((tm, ff), BF16)],
        compiler_params=_params(1),
        name="layer1_long",
    )(x, gmix, wcin, taps, state, wo, gffn, wgu, wd, gfin)


def _conv_short_kernel(x_ref, g_ref, w_ref, taps_ref, st_ref, z_ref, st_out_ref, *, seq):
    tm, d = x_ref.shape
    nseq = tm // seq
    bg, u = _conv_proj(x_ref, g_ref, w_ref)
    pos = lax.broadcasted_iota(jnp.int32, (1, seq, 1), 1)
    s0, s1 = st_ref[:, 0:1, :], st_ref[:, 1:2, :]
    u3 = u.reshape(nseq, seq, d)
    r1 = pltpu.roll(u, 1, 0).reshape(nseq, seq, d)
    r2 = pltpu.roll(u, 2, 0).reshape(nseq, seq, d)
    prev1 = jnp.where(pos == 0, s1, r1)
    prev2 = jnp.where(pos == 0, s0, jnp.where(pos == 1, s1, r2))
    conv = (taps_ref[0:1, :].reshape(1, 1, d) * prev2 + taps_ref[1:2, :].reshape(1, 1, d) * prev1
            + taps_ref[2:3, :].reshape(1, 1, d) * u3)
    z_ref[...] = (bg * conv.reshape(tm, d)).astype(BF16)
    st_out_ref[...] = u3[:, seq - (CONV_W - 1):seq, :]


def _layer1_short_kernel(x_ref, gmix_ref, wcin_ref, taps_ref, st_ref, wo_ref, gffn_ref, wgu_ref,
                         wd_ref, gfin_ref, out_ref, st_out_ref, z_ref, xn_ref, act_ref, *, seq):
    _conv_short_kernel(x_ref, gmix_ref, wcin_ref, taps_ref, st_ref, z_ref, st_out_ref, seq=seq)
    for _ in _post_pieces(x_ref, z_ref, wo_ref, gffn_ref, wgu_ref, wd_ref, gfin_ref, out_ref,
                          xn_ref, act_ref, final_norm=True):
        pass


def _layer1_short(x, gmix, wcin, taps, state, wo, gffn, wgu, wd, gfin, layer, batch, seq, tm):
    t, d = x.shape
    ff = wd.shape[1]
    keep = CONV_W - 1
    assert tm % seq == 0
    st_spec = pl.BlockSpec((tm // seq, keep, d), lambda i: (i, 0, 0))
    tile = pl.BlockSpec((tm, d), lambda i: (i, 0))
    of_layer = lambda rows, cols: pl.BlockSpec((None, rows, cols), lambda i: (layer, 0, 0),
                                               pipeline_mode=pl.Buffered(1))
    return pl.pallas_call(
        functools.partial(_layer1_short_kernel, seq=seq),
        grid=(t // tm,),
        in_specs=[tile, _resident((1, d)), _resident((d, 3 * d)), _resident((SUBLANES, d)), st_spec,
                  _resident((d, d)), _resident((1, d)), of_layer(d, 2 * ff), of_layer(ff, d),
                  _resident((1, d))],
        out_specs=[tile, st_spec],
        out_shape=[jax.ShapeDtypeStruct((t, d), F32), jax.ShapeDtypeStruct((batch, keep, d), F32)],
        scratch_shapes=[pltpu.VMEM((tm, d), BF16), pltpu.VMEM((tm, d), BF16),
                        pltpu.VMEM((tm, ff), BF16)],
        compiler_params=_params(1),
        name="layer1_short",
    )(x, gmix, wcin, taps, state, wo, gffn, wgu, wd, gfin)


def _token_tile(t):
    return min(t, 512)


def _layer1(x1, conv0, wts, batch, seq, tm):
    layer1 = _layer1_long if seq >= tm else _layer1_short
    y, conv_fin = layer1(x1, wts["norm_mix1"], wts["w_conv_in"], wts["taps"], conv0,
                         wts["w_conv_out"], wts["norm_ffn1"], wts["w_gu"], wts["w_down"],
                         wts["norm_final"], 1, batch, seq, tm)
    return y.reshape(batch, seq, -1), conv_fin[None]


def kernel(x_prompt, x_sample, state_mlstm_C, state_mlstm_n, state_mlstm_m, state_conv, norm_mix, norm_ffn, norm_final, w_mlstm_in, b_mlstm_gate, mlstm_head_norm, w_mlstm_out, w_conv_in, w_conv_taps, w_conv_out, w_ffn_gate_up, w_ffn_down):
    d = x_prompt.shape[-1]
    dk = d // N_HEADS
    assert norm_mix.shape[0] == 2 and w_mlstm_in.shape[0] == 1 and w_conv_in.shape[0] == 1
    assert w_conv_taps.shape[1] == CONV_W and 2 * N_HEADS <= LANES
    row = lambda v: v.reshape(1, -1).astype(F32)
    pad_cols = lambda m: jnp.pad(m, ((0, 0), (0, LANES - N_HEADS)))
    pad_rows = lambda m: jnp.pad(m, ((0, LANES - N_HEADS), (0, 0)))
    w_in_t = jnp.swapaxes(w_mlstm_in[0], 0, 1)
    q_scale = jnp.concatenate([jnp.full((d, 1), dk ** -0.5, F32),
                               jnp.ones((w_in_t.shape[0] - d, 1), F32)])
    w_in_bf = (w_in_t * q_scale).astype(BF16)
    wts = {
        "norm_mix0": row(norm_mix[0]), "norm_mix1": row(norm_mix[1]),
        "norm_ffn0": row(norm_ffn[0]), "norm_ffn1": row(norm_ffn[1]),
        "norm_final": row(norm_final), "g_head": row(mlstm_head_norm[0]),
        "w_in": w_in_bf,
        "w_gate": jnp.concatenate([pad_rows(w_in_bf[4 * d:4 * d + N_HEADS]),
                                   pad_rows(w_in_bf[4 * d + N_HEADS:])], axis=0),
        "b_gate": jnp.concatenate([pad_cols(b_mlstm_gate[:, :N_HEADS]),
                                   pad_cols(b_mlstm_gate[:, N_HEADS:])], axis=1).astype(F32),
        "taps": jnp.pad(w_conv_taps[0], ((0, SUBLANES - CONV_W), (0, 0))).astype(F32),
    }
    bp, sp, _ = x_prompt.shape
    bs, ss, _ = x_sample.shape
    tp, ts = bp * sp, bs * ss
    tm_p, tm_s = _token_tile(tp), _token_tile(ts)
    xp, xs = x_prompt.reshape(tp, d), x_sample.reshape(ts, d)
    n_layers, _, two_ff = w_ffn_gate_up.shape
    ff = two_ff // 2

    hg_p, c_p, n_pad, m_pad, w_gu, w_down, wts["w_conv_in"], wts["w_conv_out"], w_out0 = _mlstm_long(
        xp, wts["norm_mix0"], wts["w_in"], wts["w_gate"], wts["b_gate"], wts["g_head"], bp, sp, tm_p,
        casts=[w_ffn_gate_up.reshape(n_layers * d, two_ff), w_ffn_down.reshape(n_layers * ff, d),
               w_conv_in[0], w_conv_out[0], w_mlstm_out[0]])
    wts["w_gu"], wts["w_down"] = w_gu.reshape(n_layers, d, two_ff), w_down.reshape(n_layers, ff, d)
    n_p, m_p = n_pad[:, :N_HEADS, :], m_pad[:, 0, :N_HEADS]

    proj_s, gates_s = _inproj(xs, wts["norm_mix0"], wts["w_in"], wts["w_gate"], wts["b_gate"], tm_s)
    m0_pad = jnp.pad(state_mlstm_m[0], ((0, 0), (0, LANES - N_HEADS)))
    scan_args = (proj_s, gates_s, wts["g_head"], state_mlstm_C[0], state_mlstm_n[0], m0_pad)
    post0 = functools.partial(_post, wo=w_out0, g=wts["norm_ffn0"], wgu=wts["w_gu"], wd=wts["w_down"],
                              gfin=wts["norm_final"], layer=0)
    if bs % (tp // tm_p) == 0:
        x1_p, hg_s, c_s, n_s, m_s_pad = post0(xp, hg_p, tm=tm_p, scan=scan_args + (ss,))
    else:
        x1_p, = post0(xp, hg_p, tm=tm_p)
        hg_s, c_s, n_s, m_s_pad = _scan_sample(*scan_args, ss, 4)
    x1_s, = post0(xs, hg_s, tm=tm_s)

    y_p, conv_p = _layer1(x1_p, jnp.zeros((bp, CONV_W - 1, d), F32), wts, bp, sp, tm_p)
    y_s, conv_s = _layer1(x1_s, state_conv[0], wts, bs, ss, tm_s)
    return (y_p, y_s, c_p[None], n_p[None], m_p[None], conv_p,
            c_s[None], n_s[None], m_s_pad[:, :N_HEADS][None], conv_s)
```

```python
import functools

import jax
import jax.numpy as jnp
from jax import lax
from jax.experimental import pallas as pl
from jax.experimental.pallas import tpu as pltpu

EPS = 1e-6
N_HEADS = 4
MLSTM_CHUNK = 128
CONV_W = 3
LANES = 128
SUBLANES = 8
GATE_COLS = 2 * LANES
FFN_COLS = 256
VMEM_LIMIT = 56 * 1024 * 1024

F32 = jnp.float32
BF16 = jnp.bfloat16
_NT = (((1,), (1,)), ((), ()))
_TN = (((0,), (0,)), ((), ()))


def _params(n_grid):
    return pltpu.CompilerParams(
        dimension_semantics=("arbitrary",) * n_grid, vmem_limit_bytes=VMEM_LIMIT)


def _resident(shape):
    zeros = (0,) * len(shape)
    return pl.BlockSpec(shape, lambda *_: zeros, pipeline_mode=pl.Buffered(1))


def _rmsnorm(x, g):
    return x * lax.rsqrt(jnp.mean(x * x, axis=-1, keepdims=True) + EPS) * g


def _sigmoid(x):
    return 1.0 / (1.0 + jnp.exp(-x))


def _log_sigmoid(x):
    return jnp.minimum(x, 0.0) - jnp.log1p(jnp.exp(-jnp.abs(x)))


def _prefix(x, axis, combine, identity):
    n = x.shape[axis]
    idx = lax.broadcasted_iota(jnp.int32, x.shape, axis)
    shift = 1
    while shift < n:
        x = combine(x, jnp.where(idx >= shift, pltpu.roll(x, shift, axis), identity))
        shift *= 2
    return x


def _cumsum(x, axis):
    return _prefix(x, axis, jnp.add, 0.0)


INPROJ_COLS = 256


def _inproj_pieces(x_ref, g_ref, w_ref, wg_ref, bg_ref, proj_ref, gate_ref, *, out_in):
    d = x_ref.shape[1]
    n = w_ref.shape[0] if out_in else w_ref.shape[1]
    if out_in:
        times = lambda w, cols: lax.dot_general(xn, w[cols, :], _NT, preferred_element_type=F32)
    else:
        times = lambda w, cols: jnp.dot(xn, w[:, cols], preferred_element_type=F32)
    xn = _rmsnorm(x_ref[...], g_ref[...]).astype(BF16)
    yield
    for c0 in range(0, n, INPROJ_COLS):
        cols = slice(c0, c0 + INPROJ_COLS)
        y = times(w_ref, cols)
        if c0 >= n - d:
            y = _sigmoid(y)
        proj_ref[:, cols] = y.astype(BF16)
        yield
    gate_ref[...] = times(wg_ref, slice(0, GATE_COLS)) + bg_ref[...]
    yield


def _n_inproj_pieces(n):
    return n // INPROJ_COLS + 2


def _inproj_kernel(*refs):
    for _ in _inproj_pieces(*refs, out_in=True):
        pass


def _zip_work(primary, secondary, n_primary, n_secondary):
    next(secondary, None)
    done = 1
    for k, _ in enumerate(primary, start=1):
        while done * n_primary < k * n_secondary:
            next(secondary, None)
            done += 1
    for _ in secondary:
        pass


def _cast_specs(arr, n_steps):
    rows, cols = arr.shape
    pack = 2 * SUBLANES
    nb = max(n for n in range(1, n_steps + 1) if rows % n == 0 and (rows // n) % pack == 0)
    blk = lambda i: (jnp.minimum(i * nb // n_steps, nb - 1), 0)
    spec = pl.BlockSpec((rows // nb, cols), blk)
    return spec, spec, jax.ShapeDtypeStruct((rows, cols), BF16)


def _inproj(x, g, w, wg, bg, tm):
    t, d = x.shape
    n = 4 * d
    return pl.pallas_call(
        _inproj_kernel,
        grid=(t // tm,),
        in_specs=[pl.BlockSpec((tm, d), lambda i: (i, 0)),
                  _resident((1, d)), _resident((n, d)), _resident((GATE_COLS, d)), _resident((1, GATE_COLS))],
        out_specs=[pl.BlockSpec((tm, n), lambda i: (i, 0)),
                   pl.BlockSpec((tm, GATE_COLS), lambda i: (i, 0))],
        out_shape=[jax.ShapeDtypeStruct((t, n), BF16), jax.ShapeDtypeStruct((t, GATE_COLS), F32)],
        compiler_params=_params(1),
        name="mlstm_inproj",
    )(x, g, w, wg, bg)


def _gated_head_out(h, g_head, sig_o):
    hn = h * lax.rsqrt(jnp.mean(h * h, axis=-1, keepdims=True) + EPS)
    return hn * g_head * sig_o


def _per_row(rep, width):
    return jnp.concatenate([rep] * (width // LANES), axis=1)


def _scan_chunk(proj_ref, gate_ref, gh_ref, hg_ref, c_ref, n_ref, m_ref, r0):
    seq = MLSTM_CHUNK
    d = gh_ref.shape[1]
    dk = d // N_HEADS
    assert seq == LANES
    rows = slice(r0, r0 + seq)

    lane = lax.broadcasted_iota(jnp.int32, (1, LANES), 1)
    li = gate_ref[rows, 0:LANES]
    b = _prefix(_log_sigmoid(gate_ref[rows, LANES:GATE_COLS]), 0, jnp.add, 0.0)
    a = li - b
    m_prev = m_ref[0, 0:1, :]
    neg_mx = -jnp.maximum(m_prev, _prefix(a, 0, jnp.maximum, -jnp.inf))
    exp_neg_meff = jnp.exp(neg_mx - b)
    b_last = b[seq - 1:seq, :]
    g = b_last - b + li
    m_new = jnp.maximum(b_last + m_prev, jnp.max(g, axis=0, keepdims=True))
    decay_row = jnp.exp(b_last + m_prev - m_new)
    wg_t = jnp.exp(g - m_new).T
    a_t = a.T
    m_ref[0, 0:1, :] = jnp.where(lane < N_HEADS, m_new, 0.0)

    causal = (lax.broadcasted_iota(jnp.int32, (seq, seq), 0)
              >= lax.broadcasted_iota(jnp.int32, (seq, seq), 1))
    wg16 = wg_t[0:2 * SUBLANES, :].astype(BF16)
    den = jnp.zeros((seq, LANES), F32)
    heads = range(N_HEADS)
    tok = lambda j, h: proj_ref[rows, j * d + h * dk:j * d + (h + 1) * dk]
    yield
    p, w_rep, qx, upd = [], [], [], []
    for h in heads:
        neg_mx_rep = jnp.broadcast_to(neg_mx[:, h:h + 1], (seq, LANES))
        p.append(jnp.exp(jnp.where(causal, neg_mx_rep + a_t[h:h + 1, :], -jnp.inf)))
        w_rep.append(jnp.exp(neg_mx_rep + m_prev[:, h:h + 1]))
        rhs = jnp.concatenate([tok(1, h), c_ref[0, h].astype(BF16),
                               jnp.broadcast_to(n_ref[0, h:h + 1, :].astype(BF16), (LANES, dk))], axis=0)
        qx.append(lax.dot_general(tok(0, h), rhs, _NT, preferred_element_type=F32))
        yield
        vtw = (tok(2, h).T.astype(F32) * wg_t[h:h + 1, :]).astype(BF16)
        upd.append(jnp.dot(jnp.concatenate([vtw, wg16], axis=0), tok(1, h), preferred_element_type=F32))
        yield
    sv, s_sum = [], []
    for h in heads:
        s = qx[h][:, :seq] * p[h]
        s_sum.append(jnp.sum(s, axis=1, keepdims=True))
        sv.append(jnp.dot(s.astype(BF16), tok(2, h), preferred_element_type=F32))
        yield
    nums = []
    for h in heads:
        nums.append(_per_row(w_rep[h], dk) * qx[h][:, seq:seq + dk] + sv[h])
        den = jnp.where(lane == h, w_rep[h] * qx[h][:, seq + dk:] + s_sum[h], den)
        decay = decay_row[:, h:h + 1]
        c_ref[0, h] = decay * c_ref[0, h] + upd[h][:dk]
        n_ref[0, h:h + 1, :] = decay * n_ref[0, h:h + 1, :] + upd[h][dk + h:dk + h + 1, :]
        yield
    recip = 1.0 / jnp.maximum(jnp.abs(den), exp_neg_meff)
    yield
    hh, ssq_rep = [], []
    for h in heads:
        hh.append(nums[h] * _per_row(jnp.broadcast_to(recip[:, h:h + 1], (seq, LANES)), dk))
        ssq_rep.append(jnp.broadcast_to(jnp.sum(hh[h] * hh[h], axis=1, keepdims=True), (seq, LANES)))
        yield
    for h in heads:
        cols = slice(h * dk, (h + 1) * dk)
        hn = hh[h] * _per_row(lax.rsqrt(ssq_rep[h] * (1.0 / dk) + EPS), dk)
        hg_ref[rows, cols] = (hn * gh_ref[:, cols] * tok(3, h).astype(F32)).astype(BF16)
        yield


SCAN_CHUNK_PIECES = 6 * N_HEADS + 2


def _mlstm_long_kernel(*refs, tiles_per_seq, n_cast):
    x_ref, g_ref, w_ref, wg_ref, bg_ref, gh_ref = refs[:6]
    cast_src = refs[6:6 + n_cast]
    hg_ref, c_ref, n_ref, m_ref = refs[6 + n_cast:10 + n_cast]
    cast_dst = refs[10 + n_cast:10 + 2 * n_cast]
    proj_a, gate_a, proj_b, gate_b, w_io, wg_io = refs[10 + 2 * n_cast:]
    i = pl.program_id(0)
    tm = x_ref.shape[0]

    @pl.when(i == 0)
    def _():
        for c0 in range(0, w_ref.shape[0], INPROJ_COLS):
            w_io[:, c0:c0 + INPROJ_COLS] = w_ref[c0:c0 + INPROJ_COLS, :].T
        wg_io[...] = wg_ref[...].T

    @pl.when(lax.rem(jnp.maximum(i - 1, 0), tiles_per_seq) == 0)
    def _():
        c_ref[...] = jnp.zeros_like(c_ref)
        n_ref[...] = jnp.zeros_like(n_ref)
        m_ref[...] = jnp.zeros_like(m_ref)

    n_chunks = tm // MLSTM_CHUNK

    def scan_tile(proj_r, gate_r):
        for c in range(n_chunks):
            yield from _scan_chunk(proj_r, gate_r, gh_ref, hg_ref, c_ref, n_ref, m_ref, c * MLSTM_CHUNK)

    def project_and_cast(proj_w, gate_w):
        casts = list(zip(cast_src, cast_dst))
        for k, _ in enumerate(_inproj_pieces(x_ref, g_ref, w_io, wg_io, bg_ref, proj_w, gate_w,
                                             out_in=False)):
            if casts and k % 3 == 1:
                src, dst = casts.pop()
                dst[...] = src[...].astype(BF16)
            yield
        for src, dst in casts:
            dst[...] = src[...].astype(BF16)

    def step(proj_w, gate_w, proj_r, gate_r):
        _zip_work(scan_tile(proj_r, gate_r), project_and_cast(proj_w, gate_w),
                  n_chunks * SCAN_CHUNK_PIECES, _n_inproj_pieces(w_ref.shape[0]))

    @pl.when(i == 0)
    def _():
        for _ in project_and_cast(proj_a, gate_a):
            pass

    @pl.when(jnp.logical_and(i > 0, lax.rem(i, 2) == 0))
    def _():
        step(proj_a, gate_a, proj_b, gate_b)

    @pl.when(lax.rem(i, 2) == 1)
    def _():
        step(proj_b, gate_b, proj_a, gate_a)


def _mlstm_long(x, g, w, wg, bg, g_head, batch, seq, tm, casts):
    t, d = x.shape
    n = 4 * d
    dk = d // N_HEADS
    n_tiles = t // tm
    tiles_per_seq = seq // tm
    assert seq % tm == 0 and tm % MLSTM_CHUNK == 0
    prev = lambda i: jnp.maximum(i - 1, 0)
    cast_in, cast_out, cast_shape = zip(*[_cast_specs(a, n_tiles + 1) for a in casts])
    return pl.pallas_call(
        functools.partial(_mlstm_long_kernel, tiles_per_seq=tiles_per_seq, n_cast=len(casts)),
        grid=(n_tiles + 1,),
        in_specs=[pl.BlockSpec((tm, d), lambda i: (jnp.minimum(i, n_tiles - 1), 0)),
                  _resident((1, d)), _resident((n, d)), _resident((GATE_COLS, d)),
                  _resident((1, GATE_COLS)), _resident((1, d)), *cast_in],
        out_specs=[pl.BlockSpec((tm, d), lambda i: (prev(i), 0)),
                   pl.BlockSpec((1, N_HEADS, dk, dk), lambda i: (prev(i) // tiles_per_seq, 0, 0, 0)),
                   pl.BlockSpec((1, SUBLANES, dk), lambda i: (prev(i) // tiles_per_seq, 0, 0)),
                   pl.BlockSpec((1, SUBLANES, LANES), lambda i: (prev(i) // tiles_per_seq, 0, 0)),
                   *cast_out],
        out_shape=[jax.ShapeDtypeStruct((t, d), BF16),
                   jax.ShapeDtypeStruct((batch, N_HEADS, dk, dk), F32),
                   jax.ShapeDtypeStruct((batch, SUBLANES, dk), F32),
                   jax.ShapeDtypeStruct((batch, SUBLANES, LANES), F32), *cast_shape],
        scratch_shapes=[pltpu.VMEM((tm, n), BF16), pltpu.VMEM((tm, GATE_COLS), F32),
                        pltpu.VMEM((tm, n), BF16), pltpu.VMEM((tm, GATE_COLS), F32),
                        pltpu.VMEM((d, n), BF16), pltpu.VMEM((d, GATE_COLS), BF16)],
        compiler_params=_params(1),
        name="mlstm_long",
    )(x, g, w, wg, bg, g_head, *casts)


SCAN_SAMPLE_GROUP = 4


def _n_scan_sample_pieces(bt):
    return 1 + 4 * (bt * N_HEADS // SCAN_SAMPLE_GROUP)


def _scan_sample_pieces(q_ref, k_ref, v_ref, so_ref, gate_ref, gh_ref, c_in, n_in, m_in,
                        hg_ref, c_out, n_out, m_out, *, seq):
    bt = c_in.shape[0]
    dk = q_ref.shape[1] // N_HEADS
    step = pl.program_id(0)
    row = lax.broadcasted_iota(jnp.int32, (seq, seq), 0)
    col = lax.broadcasted_iota(jnp.int32, (seq, seq), 1)
    causal = row >= col
    diag = row == col
    lane = lax.broadcasted_iota(jnp.int32, (1, LANES), 1)
    to_row = lambda x_col: jnp.sum(jnp.where(diag, x_col, 0.0), axis=0, keepdims=True)

    qs, ks, vs = q_ref[...].astype(F32), k_ref[...].astype(F32), v_ref[...].astype(F32)
    sig_o = so_ref[...].astype(F32)
    pairs = [(j, h) for j in range(bt) for h in range(N_HEADS)]
    tok = lambda x, j, h: x[j * seq:(j + 1) * seq, h * dk:(h + 1) * dk]

    gate_q = {}
    for j in range(bt):
        gates = gate_ref[j * seq:(j + 1) * seq, :]
        b_col = _cumsum(_log_sigmoid(gates), 0)
        m_prev_row = m_in[pl.ds(step * bt + j, 1), :]
        m_new_row = jnp.zeros((1, LANES), F32)
        for h in range(N_HEADS):
            li_c, b_c = gates[:, h:h + 1], b_col[:, LANES + h:LANES + h + 1]
            m_prev = m_prev_row[:, h:h + 1]
            b_last = b_c[seq - 1:seq, :]
            dm = jnp.where(causal, b_c - to_row(b_c) + to_row(li_c), -jnp.inf)
            inter = b_c + m_prev
            m_eff = jnp.maximum(inter, jnp.max(dm, axis=1, keepdims=True))
            g = b_last - b_c + li_c
            m_new = jnp.maximum(b_last + m_prev, jnp.max(g, axis=0, keepdims=True))
            gate_q[j, h] = dict(w_inter=jnp.exp(inter - m_eff), p=jnp.exp(dm - m_eff),
                                floor=jnp.exp(-m_eff), decay=jnp.exp(b_last + m_prev - m_new),
                                wg=jnp.exp(g - m_new))
            m_new_row = jnp.where(lane == h, m_new, m_new_row)
        m_out[pl.ds(step * bt + j, 1), :] = m_new_row
    yield

    def grouped(seq_of_pairs):
        for idx, pair in enumerate(seq_of_pairs, start=1):
            yield pair, idx % SCAN_SAMPLE_GROUP == 0

    pack = 2 * SUBLANES
    assert seq <= pack
    as_tile = lambda a: jnp.concatenate(
        [a, jnp.zeros((pack - a.shape[0], a.shape[1]), F32)], axis=0).astype(BF16)
    qx = {}
    for (j, h), pause in grouped(pairs):
        rhs = jnp.concatenate([c_in[j, h].astype(BF16), as_tile(tok(ks, j, h)),
                               as_tile(n_in[j, h:h + 1, :])], axis=0)
        qx[j, h] = lax.dot_general(as_tile(tok(qs, j, h)), rhs, _NT,
                                   preferred_element_type=F32)[:seq]
        if pause:
            yield

    sv, s_sum = {}, {}
    for (j, h), pause in grouped(pairs):
        s = qx[j, h][:, dk:dk + seq] * gate_q[j, h]["p"]
        v = tok(vs, j, h)
        acc = s[:, 0:1] * v[0:1, :]
        for t in range(1, seq):
            acc = acc + s[:, t:t + 1] * v[t:t + 1, :]
        sv[j, h], s_sum[j, h] = acc, jnp.sum(s, axis=1, keepdims=True)
        if pause:
            yield

    for (j, h), pause in grouped(pairs):
        gq = gate_q[j, h]
        num = gq["w_inter"] * qx[j, h][:, :dk] + sv[j, h]
        den = gq["w_inter"] * qx[j, h][:, dk + pack:dk + pack + 1] + s_sum[j, h]
        hh = num * (1.0 / jnp.maximum(jnp.abs(den), gq["floor"]))
        hg_ref[j * seq:(j + 1) * seq, h * dk:(h + 1) * dk] = _gated_head_out(
            hh, gh_ref[:, h * dk:(h + 1) * dk], tok(sig_o, j, h))
        if pause:
            yield

    def update_of(j, h):
        return lax.dot_general(as_tile(tok(vs, j, h) * gate_q[j, h]["wg"]), as_tile(tok(ks, j, h)),
                               _TN, preferred_element_type=F32)

    upd_next = update_of(*pairs[0])
    for idx, ((j, h), pause) in enumerate(grouped(pairs)):
        upd = upd_next
        if idx + 1 < len(pairs):
            upd_next = update_of(*pairs[idx + 1])
        gq = gate_q[j, h]
        c_out[j, h] = gq["decay"] * c_in[j, h] + upd
        n_out[j, h:h + 1, :] = (gq["decay"] * n_in[j, h:h + 1, :]
                                + jnp.sum(tok(ks, j, h) * gq["wg"], axis=0, keepdims=True))
        if pause:
            yield


def _scan_sample_kernel(*refs, seq):
    for _ in _scan_sample_pieces(*refs, seq=seq):
        pass


def _scan_sample_specs(g_head, c0, n0, seq, bt):
    batch = c0.shape[0]
    d = g_head.shape[1]
    dk = d // N_HEADS
    tok = lambda j: pl.BlockSpec((bt * seq, d), lambda i: (i, j))
    c_spec = pl.BlockSpec((bt, N_HEADS, dk, dk), lambda i: (i, 0, 0, 0))
    n_spec = pl.BlockSpec((bt, N_HEADS, dk), lambda i: (i, 0, 0))
    m_spec = pl.BlockSpec((batch, LANES), lambda i: (0, 0))
    in_specs = [tok(0), tok(1), tok(2), tok(3),
                pl.BlockSpec((bt * seq, GATE_COLS), lambda i: (i, 0)),
                _resident((1, d)), c_spec, n_spec, m_spec]
    out_shapes = [jax.ShapeDtypeStruct((batch * seq, d), F32),
                  jax.ShapeDtypeStruct(c0.shape, F32),
                  jax.ShapeDtypeStruct(n0.shape, F32),
                  jax.ShapeDtypeStruct((batch, LANES), F32)]
    return in_specs, [tok(0), c_spec, n_spec, m_spec], out_shapes


def _scan_sample(proj, gates, g_head, c0, n0, m0, seq, bt):
    in_specs, out_specs, out_shapes = _scan_sample_specs(g_head, c0, n0, seq, bt)
    return pl.pallas_call(
        functools.partial(_scan_sample_kernel, seq=seq),
        grid=(c0.shape[0] // bt,),
        in_specs=in_specs, out_specs=out_specs, out_shape=out_shapes,
        compiler_params=_params(1),
        name="mlstm_scan_sample",
    )(proj, proj, proj, proj, gates, g_head, c0, n0, m0)


def _post_pieces(x_ref, z_ref, wo_ref, g_ref, wgu_ref, wd_ref, gfin_ref, out_ref,
                 xn_ref, act_ref, *, final_norm):
    x1 = x_ref[...] + jnp.dot(z_ref[...].astype(BF16), wo_ref[...], preferred_element_type=F32)
    out_ref[...] = x1
    xn_ref[...] = _rmsnorm(x1, g_ref[...]).astype(BF16)
    yield
    ff = wd_ref.shape[0]
    for c in range(ff // FFN_COLS):
        gate = jnp.dot(xn_ref[...], wgu_ref[:, c * FFN_COLS:(c + 1) * FFN_COLS],
                       preferred_element_type=F32)
        up = jnp.dot(xn_ref[...], wgu_ref[:, ff + c * FFN_COLS:ff + (c + 1) * FFN_COLS],
                     preferred_element_type=F32)
        act_ref[:, c * FFN_COLS:(c + 1) * FFN_COLS] = (gate * _sigmoid(gate) * up).astype(BF16)
        yield
    x2 = out_ref[...] + jnp.dot(act_ref[...], wd_ref[...], preferred_element_type=F32)
    if final_norm:
        x2 = _rmsnorm(x2, gfin_ref[...])
    out_ref[...] = x2
    yield


def _n_post_pieces(ff):
    return ff // FFN_COLS + 2


def _post_kernel(*refs, scan_seq):
    if scan_seq is None:
        for _ in _post_pieces(*refs, final_norm=False):
            pass
        return
    post_in, scan_in = refs[:7], refs[7:16]
    out_ref, scan_out = refs[16], refs[17:21]
    xn_ref, act_ref = refs[21:]
    _zip_work(_post_pieces(*post_in, out_ref, xn_ref, act_ref, final_norm=False),
              _scan_sample_pieces(*scan_in, *scan_out, seq=scan_seq),
              _n_post_pieces(post_in[5].shape[0]), _n_scan_sample_pieces(scan_in[6].shape[0]))


def _post(x, z, wo, g, wgu, wd, gfin, layer, tm, scan=None):
    t, d = x.shape
    ff = wd.shape[1]
    assert ff % FFN_COLS == 0
    n_steps = t // tm
    tile = pl.BlockSpec((tm, d), lambda i: (i, 0))
    of_layer = lambda rows, cols: pl.BlockSpec((None, rows, cols), lambda i: (layer, 0, 0),
                                               pipeline_mode=pl.Buffered(1))
    in_specs = [tile, tile, _resident((d, d)), _resident((1, d)),
                of_layer(d, 2 * ff), of_layer(ff, d), _resident((1, d))]
    out_specs, out_shapes, operands = [tile], [jax.ShapeDtypeStruct((t, d), F32)], [x, z, wo, g, wgu, wd, gfin]
    scan_seq = None
    if scan is not None:
        proj, gates, g_head, c0, n0, m0, scan_seq = scan
        assert c0.shape[0] % n_steps == 0
        s_in, s_out, s_shapes = _scan_sample_specs(g_head, c0, n0, scan_seq, c0.shape[0] // n_steps)
        in_specs, out_specs, out_shapes = in_specs + s_in, out_specs + s_out, out_shapes + s_shapes
        operands += [proj, proj, proj, proj, gates, g_head, c0, n0, m0]
    return pl.pallas_call(
        functools.partial(_post_kernel, scan_seq=scan_seq),
        grid=(n_steps,),
        in_specs=in_specs, out_specs=out_specs, out_shape=out_shapes,
        scratch_shapes=[pltpu.VMEM((tm, d), BF16), pltpu.VMEM((tm, ff), BF16)],
        compiler_params=_params(1),
        name="outproj_ffn",
    )(*operands)


def _conv_proj(x_ref, g_ref, w_ref):
    d = x_ref.shape[1]
    xn = _rmsnorm(x_ref[...], g_ref[...]).astype(BF16)
    part = lambda c: jnp.dot(xn, w_ref[:, c * d:(c + 1) * d], preferred_element_type=F32)
    return part(0), part(1) * part(2)


def _conv_long_pieces(x_ref, g_ref, w_ref, taps_ref, st_out_ref, carry_ref, z_ref, x_copy_ref):
    tm, d = x_ref.shape
    x = x_ref[...]
    x_copy_ref[...] = x
    xn = _rmsnorm(x, g_ref[...]).astype(BF16)
    row = lax.broadcasted_iota(jnp.int32, (tm, 1), 0)
    yield
    for c0 in range(0, d, INPROJ_COLS):
        cols = slice(c0, c0 + INPROJ_COLS)
        part = lambda k: jnp.dot(xn, w_ref[:, k * d + c0:k * d + c0 + INPROJ_COLS],
                                 preferred_element_type=F32)
        bg, u = part(0), part(1) * part(2)
        yield
        s0, s1 = carry_ref[0:1, cols], carry_ref[1:2, cols]
        prev1 = jnp.where(row == 0, s1, pltpu.roll(u, 1, 0))
        prev2 = jnp.where(row == 0, s0, jnp.where(row == 1, s1, pltpu.roll(u, 2, 0)))
        conv = taps_ref[0:1, cols] * prev2 + taps_ref[1:2, cols] * prev1 + taps_ref[2:3, cols] * u
        z_ref[:, cols] = (bg * conv).astype(BF16)
        tail = u[tm - (CONV_W - 1):tm, :]
        carry_ref[:, cols] = tail
        st_out_ref[0, :, cols] = tail
        yield


def _n_conv_long_pieces(d):
    return 1 + 2 * (d // INPROJ_COLS)


def _layer1_long_kernel(x_ref, gmix_ref, wcin_ref, taps_ref, st_ref, wo_ref, gffn_ref, wgu_ref,
                        wd_ref, gfin_ref, out_ref, st_out_ref,
                        carry_ref, z_a, x_a, z_b, x_b, xn_ref, act_ref, *, tiles_per_seq, n_tiles):
    i = pl.program_id(0)

    @pl.when(jnp.logical_and(i < n_tiles, lax.rem(i, tiles_per_seq) == 0))
    def _():
        carry_ref[...] = st_ref[0]

    def conv_tile(z_w, x_w):
        return _conv_long_pieces(x_ref, gmix_ref, wcin_ref, taps_ref, st_out_ref, carry_ref, z_w, x_w)

    def post_tile(z_r, x_r):
        return _post_pieces(x_r, z_r, wo_ref, gffn_ref, wgu_ref, wd_ref, gfin_ref, out_ref,
                            xn_ref, act_ref, final_norm=True)

    def step(z_w, x_w, z_r, x_r):
        _zip_work(post_tile(z_r, x_r), conv_tile(z_w, x_w),
                  _n_post_pieces(wd_ref.shape[0]), _n_conv_long_pieces(x_ref.shape[1]))

    @pl.when(i == 0)
    def _():
        for _ in conv_tile(z_a, x_a):
            pass

    @pl.when(jnp.logical_and(i > 0, lax.rem(i, 2) == 0))
    def _():
        step(z_a, x_a, z_b, x_b)

    @pl.when(lax.rem(i, 2) == 1)
    def _():
        step(z_b, x_b, z_a, x_a)


def _layer1_long(x, gmix, wcin, taps, state, wo, gffn, wgu, wd, gfin, layer, batch, seq, tm):
    t, d = x.shape
    ff = wd.shape[1]
    keep = CONV_W - 1
    n_tiles = t // tm
    tiles_per_seq = seq // tm
    assert seq % tm == 0
    cur = lambda i: jnp.minimum(i, n_tiles - 1)
    st_spec = pl.BlockSpec((1, keep, d), lambda i: (cur(i) // tiles_per_seq, 0, 0))
    of_layer = lambda rows, cols: pl.BlockSpec((None, rows, cols), lambda i: (layer, 0, 0),
                                               pipeline_mode=pl.Buffered(1))
    return pl.pallas_call(
        functools.partial(_layer1_long_kernel, tiles_per_seq=tiles_per_seq, n_tiles=n_tiles),
        grid=(n_tiles + 1,),
        in_specs=[pl.BlockSpec((tm, d), lambda i: (cur(i), 0)),
                  _resident((1, d)), _resident((d, 3 * d)), _resident((SUBLANES, d)), st_spec,
                  _resident((d, d)), _resident((1, d)), of_layer(d, 2 * ff), of_layer(ff, d),
                  _resident((1, d))],
        out_specs=[pl.BlockSpec((tm, d), lambda i: (jnp.maximum(i - 1, 0), 0)), st_spec],
        out_shape=[jax.ShapeDtypeStruct((t, d), F32), jax.ShapeDtypeStruct((batch, keep, d), F32)],
        scratch_shapes=[pltpu.VMEM((keep, d), F32),
                        pltpu.VMEM((tm, d), BF16), pltpu.VMEM((tm, d), F32),
                        pltpu.VMEM((tm, d), BF16), pltpu.VMEM((tm, d), F32),
                        pltpu.VMEM((tm, d), BF16), pltpu.VMEM((tm, ff), BF16)],
        compiler_params=_params(1),
        name="layer1_long",
    )(x, gmix, wcin, taps, state, wo, gffn, wgu, wd, gfin)


def _conv_short_kernel(x_ref, g_ref, w_ref, taps_ref, st_ref, z_ref, st_out_ref, *, seq):
    tm, d = x_ref.shape
    nseq = tm // seq
    bg, u = _conv_proj(x_ref, g_ref, w_ref)
    pos = lax.broadcasted_iota(jnp.int32, (1, seq, 1), 1)
    s0, s1 = st_ref[:, 0:1, :], st_ref[:, 1:2, :]
    u3 = u.reshape(nseq, seq, d)
    r1 = pltpu.roll(u, 1, 0).reshape(nseq, seq, d)
    r2 = pltpu.roll(u, 2, 0).reshape(nseq, seq, d)
    prev1 = jnp.where(pos == 0, s1, r1)
    prev2 = jnp.where(pos == 0, s0, jnp.where(pos == 1, s1, r2))
    conv = (taps_ref[0:1, :].reshape(1, 1, d) * prev2 + taps_ref[1:2, :].reshape(1, 1, d) * prev1
            + taps_ref[2:3, :].reshape(1, 1, d) * u3)
    z_ref[...] = (bg * conv.reshape(tm, d)).astype(BF16)
    st_out_ref[...] = u3[:, seq - (CONV_W - 1):seq, :]


def _layer1_short_kernel(x_ref, gmix_ref, wcin_ref, taps_ref, st_ref, wo_ref, gffn_ref, wgu_ref,
                         wd_ref, gfin_ref, out_ref, st_out_ref, z_ref, xn_ref, act_ref, *, seq):
    _conv_short_kernel(x_ref, gmix_ref, wcin_ref, taps_ref, st_ref, z_ref, st_out_ref, seq=seq)
    for _ in _post_pieces(x_ref, z_ref, wo_ref, gffn_ref, wgu_ref, wd_ref, gfin_ref, out_ref,
                          xn_ref, act_ref, final_norm=True):
        pass


def _layer1_short(x, gmix, wcin, taps, state, wo, gffn, wgu, wd, gfin, layer, batch, seq, tm):
    t, d = x.shape
    ff = wd.shape[1]
    keep = CONV_W - 1
    assert tm % seq == 0
    st_spec = pl.BlockSpec((tm // seq, keep, d), lambda i: (i, 0, 0))
    tile = pl.BlockSpec((tm, d), lambda i: (i, 0))
    of_layer = lambda rows, cols: pl.BlockSpec((None, rows, cols), lambda i: (layer, 0, 0),
                                               pipeline_mode=pl.Buffered(1))
    return pl.pallas_call(
        functools.partial(_layer1_short_kernel, seq=seq),
        grid=(t // tm,),
        in_specs=[tile, _resident((1, d)), _resident((d, 3 * d)), _resident((SUBLANES, d)), st_spec,
                  _resident((d, d)), _resident((1, d)), of_layer(d, 2 * ff), of_layer(ff, d),
                  _resident((1, d))],
        out_specs=[tile, st_spec],
        out_shape=[jax.ShapeDtypeStruct((t, d), F32), jax.ShapeDtypeStruct((batch, keep, d), F32)],
        scratch_shapes=[pltpu.VMEM((tm, d), BF16), pltpu.VMEM((tm, d), BF16),
                        pltpu.VMEM((tm, ff), BF16)],
        compiler_params=_params(1),
        name="layer1_short",
    )(x, gmix, wcin, taps, state, wo, gffn, wgu, wd, gfin)


def _token_tile(t):
    return min(t, 512)


def _layer1(x1, conv0, wts, batch, seq, tm):
    layer1 = _layer1_long if seq >= tm else _layer1_short
    y, conv_fin = layer1(x1, wts["norm_mix1"], wts["w_conv_in"], wts["taps"], conv0,
                         wts["w_conv_out"], wts["norm_ffn1"], wts["w_gu"], wts["w_down"],
                         wts["norm_final"], 1, batch, seq, tm)
    return y.reshape(batch, seq, -1), conv_fin[None]


def kernel(x_prompt, x_sample, state_mlstm_C, state_mlstm_n, state_mlstm_m, state_conv, norm_mix, norm_ffn, norm_final, w_mlstm_in, b_mlstm_gate, mlstm_head_norm, w_mlstm_out, w_conv_in, w_conv_taps, w_conv_out, w_ffn_gate_up, w_ffn_down):
    d = x_prompt.shape[-1]
    dk = d // N_HEADS
    assert norm_mix.shape[0] == 2 and w_mlstm_in.shape[0] == 1 and w_conv_in.shape[0] == 1
    assert w_conv_taps.shape[1] == CONV_W and 2 * N_HEADS <= LANES
    row = lambda v: v.reshape(1, -1).astype(F32)
    pad_cols = lambda m: jnp.pad(m, ((0, 0), (0, LANES - N_HEADS)))
    pad_rows = lambda m: jnp.pad(m, ((0, LANES - N_HEADS), (0, 0)))
    w_in_t = jnp.swapaxes(w_mlstm_in[0], 0, 1)
    q_scale = jnp.concatenate([jnp.full((d, 1), dk ** -0.5, F32),
                               jnp.ones((w_in_t.shape[0] - d, 1), F32)])
    w_in_bf = (w_in_t * q_scale).astype(BF16)
    wts = {
        "norm_mix0": row(norm_mix[0]), "norm_mix1": row(norm_mix[1]),
        "norm_ffn0": row(norm_ffn[0]), "norm_ffn1": row(norm_ffn[1]),
        "norm_final": row(norm_final), "g_head": row(mlstm_head_norm[0]),
        "w_in": w_in_bf,
        "w_gate": jnp.concatenate([pad_rows(w_in_bf[4 * d:4 * d + N_HEADS]),
                                   pad_rows(w_in_bf[4 * d + N_HEADS:])], axis=0),
        "b_gate": jnp.concatenate([pad_cols(b_mlstm_gate[:, :N_HEADS]),
                                   pad_cols(b_mlstm_gate[:, N_HEADS:])], axis=1).astype(F32),
        "taps": jnp.pad(w_conv_taps[0], ((0, SUBLANES - CONV_W), (0, 0))).astype(F32),
    }
    bp, sp, _ = x_prompt.shape
    bs, ss, _ = x_sample.shape
    tp, ts = bp * sp, bs * ss
    tm_p, tm_s = _token_tile(tp), _token_tile(ts)
    xp, xs = x_prompt.reshape(tp, d), x_sample.reshape(ts, d)
    n_layers, _, two_ff = w_ffn_gate_up.shape
    ff = two_ff // 2

    hg_p, c_p, n_pad, m_pad, w_gu, w_down, wts["w_conv_in"], wts["w_conv_out"], w_out0 = _mlstm_long(
        xp, wts["norm_mix0"], wts["w_in"], wts["w_gate"], wts["b_gate"], wts["g_head"], bp, sp, tm_p,
        casts=[w_ffn_gate_up.reshape(n_layers * d, two_ff), w_ffn_down.reshape(n_layers * ff, d),
               w_conv_in[0], w_conv_out[0], w_mlstm_out[0]])
    wts["w_gu"], wts["w_down"] = w_gu.reshape(n_layers, d, two_ff), w_down.reshape(n_layers, ff, d)
    n_p, m_p = n_pad[:, :N_HEADS, :], m_pad[:, 0, :N_HEADS]

    proj_s, gates_s = _inproj(xs, wts["norm_mix0"], wts["w_in"], wts["w_gate"], wts["b_gate"], tm_s)
    m0_pad = jnp.pad(state_mlstm_m[0], ((0, 0), (0, LANES - N_HEADS)))
    scan_args = (proj_s, gates_s, wts["g_head"], state_mlstm_C[0], state_mlstm_n[0], m0_pad)
    post0 = functools.partial(_post, wo=w_out0, g=wts["norm_ffn0"], wgu=wts["w_gu"], wd=wts["w_down"],
                              gfin=wts["norm_final"], layer=0)
    if bs % (tp // tm_p) == 0:
        x1_p, hg_s, c_s, n_s, m_s_pad = post0(xp, hg_p, tm=tm_p, scan=scan_args + (ss,))
    else:
        x1_p, = post0(xp, hg_p, tm=tm_p)
        hg_s, c_s, n_s, m_s_pad = _scan_sample(*scan_args, ss, 4)
    x1_s, = post0(xs, hg_s, tm=tm_s)

    y_p, conv_p = _layer1(x1_p, jnp.zeros((bp, CONV_W - 1, d), F32), wts, bp, sp, tm_p)
    y_s, conv_s = _layer1(x1_s, state_conv[0], wts, bs, ss, tm_s)
    return (y_p, y_s, c_p[None], n_p[None], m_p[None], conv_p,
            c_s[None], n_s[None], m_s_pad[:, :N_HEADS][None], conv_s)
```

```python
import functools

import jax
import jax.numpy as jnp
from jax import lax
from jax.experimental import pallas as pl
from jax.experimental.pallas import tpu as pltpu

EPS = 1e-6
N_HEADS = 4
MLSTM_CHUNK = 128
CONV_W = 3
LANES = 128
SUBLANES = 8
GATE_COLS = 2 * LANES
FFN_COLS = 256
VMEM_LIMIT = 56 * 1024 * 1024

F32 = jnp.float32
BF16 = jnp.bfloat16
_NT = (((1,), (1,)), ((), ()))
_TN = (((0,), (0,)), ((), ()))


def _params(n_grid):
    return pltpu.CompilerParams(
        dimension_semantics=("arbitrary",) * n_grid, vmem_limit_bytes=VMEM_LIMIT)


def _resident(shape):
    zeros = (0,) * len(shape)
    return pl.BlockSpec(shape, lambda *_: zeros, pipeline_mode=pl.Buffered(1))


def _rmsnorm(x, g):
    return x * lax.rsqrt(jnp.mean(x * x, axis=-1, keepdims=True) + EPS) * g


def _sigmoid(x):
    return 1.0 / (1.0 + jnp.exp(-x))


def _log_sigmoid(x):
    return jnp.minimum(x, 0.0) - jnp.log1p(jnp.exp(-jnp.abs(x)))


def _prefix(x, axis, combine, identity):
    n = x.shape[axis]
    idx = lax.broadcasted_iota(jnp.int32, x.shape, axis)
    shift = 1
    while shift < n:
        x = combine(x, jnp.where(idx >= shift, pltpu.roll(x, shift, axis), identity))
        shift *= 2
    return x


def _cumsum(x, axis):
    return _prefix(x, axis, jnp.add, 0.0)


INPROJ_COLS = 256


def _inproj_pieces(x_ref, g_ref, w_ref, wg_ref, bg_ref, proj_ref, gate_ref, *, out_in):
    d = x_ref.shape[1]
    n = w_ref.shape[0] if out_in else w_ref.shape[1]
    if out_in:
        times = lambda w, cols: lax.dot_general(xn, w[cols, :], _NT, preferred_element_type=F32)
    else:
        times = lambda w, cols: jnp.dot(xn, w[:, cols], preferred_element_type=F32)
    xn = _rmsnorm(x_ref[...], g_ref[...]).astype(BF16)
    yield
    for c0 in range(0, n, INPROJ_COLS):
        cols = slice(c0, c0 + INPROJ_COLS)
        y = times(w_ref, cols)
        if c0 >= n - d:
            y = _sigmoid(y)
        proj_ref[:, cols] = y.astype(BF16)
        yield
    gate_ref[...] = times(wg_ref, slice(0, GATE_COLS)) + bg_ref[...]
    yield


def _n_inproj_pieces(n):
    return n // INPROJ_COLS + 2


def _inproj_kernel(*refs):
    for _ in _inproj_pieces(*refs, out_in=True):
        pass


def _zip_work(primary, secondary, n_primary, n_secondary, span_pct=100):
    n_primary = (n_primary * span_pct) // 100
    next(secondary, None)
    done = 1
    for k, _ in enumerate(primary, start=1):
        while done * n_primary < k * n_secondary:
            next(secondary, None)
            done += 1
    for _ in secondary:
        pass


def _cast_specs(arr, n_steps):
    rows, cols = arr.shape
    pack = 2 * SUBLANES
    nb = max(n for n in range(1, n_steps + 1) if rows % n == 0 and (rows // n) % pack == 0)
    blk = lambda i: (jnp.minimum(i * nb // n_steps, nb - 1), 0)
    spec = pl.BlockSpec((rows // nb, cols), blk)
    return spec, spec, jax.ShapeDtypeStruct((rows, cols), BF16)


def _inproj(x, g, w, wg, bg, tm):
    t, d = x.shape
    n = 4 * d
    return pl.pallas_call(
        _inproj_kernel,
        grid=(t // tm,),
        in_specs=[pl.BlockSpec((tm, d), lambda i: (i, 0)),
                  _resident((1, d)), _resident((n, d)), _resident((GATE_COLS, d)), _resident((1, GATE_COLS))],
        out_specs=[pl.BlockSpec((tm, n), lambda i: (i, 0)),
                   pl.BlockSpec((tm, GATE_COLS), lambda i: (i, 0))],
        out_shape=[jax.ShapeDtypeStruct((t, n), BF16), jax.ShapeDtypeStruct((t, GATE_COLS), F32)],
        compiler_params=_params(1),
        name="mlstm_inproj",
    )(x, g, w, wg, bg)


def _gated_head_out(h, g_head, sig_o):
    hn = h * lax.rsqrt(jnp.mean(h * h, axis=-1, keepdims=True) + EPS)
    return hn * g_head * sig_o


def _per_row(rep, width):
    return jnp.concatenate([rep] * (width // LANES), axis=1)


def _scan_chunk(proj_ref, gate_ref, gh_ref, hg_ref, c_ref, n_ref, m_ref, r0):
    seq = MLSTM_CHUNK
    d = gh_ref.shape[1]
    dk = d // N_HEADS
    assert seq == LANES
    rows = slice(r0, r0 + seq)

    lane = lax.broadcasted_iota(jnp.int32, (1, LANES), 1)
    li = gate_ref[rows, 0:LANES]
    b = _prefix(_log_sigmoid(gate_ref[rows, LANES:GATE_COLS]), 0, jnp.add, 0.0)
    a = li - b
    m_prev = m_ref[0, 0:1, :]
    neg_mx = -jnp.maximum(m_prev, _prefix(a, 0, jnp.maximum, -jnp.inf))
    exp_neg_meff = jnp.exp(neg_mx - b)
    b_last = b[seq - 1:seq, :]
    g = b_last - b + li
    m_new = jnp.maximum(b_last + m_prev, jnp.max(g, axis=0, keepdims=True))
    decay_row = jnp.exp(b_last + m_prev - m_new)
    wg_t = jnp.exp(g - m_new).T
    a_t = a.T
    m_ref[0, 0:1, :] = jnp.where(lane < N_HEADS, m_new, 0.0)

    causal = (lax.broadcasted_iota(jnp.int32, (seq, seq), 0)
              >= lax.broadcasted_iota(jnp.int32, (seq, seq), 1))
    wg16 = wg_t[0:2 * SUBLANES, :].astype(BF16)
    den = jnp.zeros((seq, LANES), F32)
    heads = range(N_HEADS)
    tok = lambda j, h: proj_ref[rows, j * d + h * dk:j * d + (h + 1) * dk]
    yield
    p, w_rep, qx, upd = [], [], [], []
    for h in heads:
        neg_mx_rep = jnp.broadcast_to(neg_mx[:, h:h + 1], (seq, LANES))
        p.append(jnp.exp(jnp.where(causal, neg_mx_rep + a_t[h:h + 1, :], -jnp.inf)))
        w_rep.append(jnp.exp(neg_mx_rep + m_prev[:, h:h + 1]))
        rhs = jnp.concatenate([tok(1, h), c_ref[0, h].astype(BF16),
                               jnp.broadcast_to(n_ref[0, h:h + 1, :].astype(BF16), (LANES, dk))], axis=0)
        qx.append(lax.dot_general(tok(0, h), rhs, _NT, preferred_element_type=F32))
        yield
        vtw = (tok(2, h).T.astype(F32) * wg_t[h:h + 1, :]).astype(BF16)
        upd.append(jnp.dot(jnp.concatenate([vtw, wg16], axis=0), tok(1, h), preferred_element_type=F32))
        yield
    sv, s_sum = [], []
    for h in heads:
        s = qx[h][:, :seq] * p[h]
        s_sum.append(jnp.sum(s, axis=1, keepdims=True))
        sv.append(jnp.dot(s.astype(BF16), tok(2, h), preferred_element_type=F32))
        yield
    nums = []
    for h in heads:
        nums.append(_per_row(w_rep[h], dk) * qx[h][:, seq:seq + dk] + sv[h])
        den = jnp.where(lane == h, w_rep[h] * qx[h][:, seq + dk:] + s_sum[h], den)
        decay = decay_row[:, h:h + 1]
        c_ref[0, h] = decay * c_ref[0, h] + upd[h][:dk]
        n_ref[0, h:h + 1, :] = decay * n_ref[0, h:h + 1, :] + upd[h][dk + h:dk + h + 1, :]
        yield
    recip = 1.0 / jnp.maximum(jnp.abs(den), exp_neg_meff)
    yield
    hh, ssq_rep = [], []
    for h in heads:
        hh.append(nums[h] * _per_row(jnp.broadcast_to(recip[:, h:h + 1], (seq, LANES)), dk))
        ssq_rep.append(jnp.broadcast_to(jnp.sum(hh[h] * hh[h], axis=1, keepdims=True), (seq, LANES)))
        yield
    for h in heads:
        cols = slice(h * dk, (h + 1) * dk)
        hn = hh[h] * _per_row(lax.rsqrt(ssq_rep[h] * (1.0 / dk) + EPS), dk)
        hg_ref[rows, cols] = (hn * gh_ref[:, cols] * tok(3, h).astype(F32)).astype(BF16)
        yield


SCAN_CHUNK_PIECES = 6 * N_HEADS + 2
PROJ_SPAN_PCT = 85


def _mlstm_long_kernel(*refs, tiles_per_seq, n_cast):
    x_ref, g_ref, w_ref, wg_ref, bg_ref, gh_ref = refs[:6]
    cast_src = refs[6:6 + n_cast]
    hg_ref, c_ref, n_ref, m_ref = refs[6 + n_cast:10 + n_cast]
    cast_dst = refs[10 + n_cast:10 + 2 * n_cast]
    proj_a, gate_a, proj_b, gate_b, w_io, wg_io = refs[10 + 2 * n_cast:]
    i = pl.program_id(0)
    tm = x_ref.shape[0]

    @pl.when(i == 0)
    def _():
        for c0 in range(0, w_ref.shape[0], INPROJ_COLS):
            w_io[:, c0:c0 + INPROJ_COLS] = w_ref[c0:c0 + INPROJ_COLS, :].T
        wg_io[...] = wg_ref[...].T

    @pl.when(lax.rem(jnp.maximum(i - 1, 0), tiles_per_seq) == 0)
    def _():
        c_ref[...] = jnp.zeros_like(c_ref)
        n_ref[...] = jnp.zeros_like(n_ref)
        m_ref[...] = jnp.zeros_like(m_ref)

    n_chunks = tm // MLSTM_CHUNK

    def scan_tile(proj_r, gate_r):
        for c in range(n_chunks):
            yield from _scan_chunk(proj_r, gate_r, gh_ref, hg_ref, c_ref, n_ref, m_ref, c * MLSTM_CHUNK)

    def project_and_cast(proj_w, gate_w):
        casts = list(zip(cast_src, cast_dst))
        for k, _ in enumerate(_inproj_pieces(x_ref, g_ref, w_io, wg_io, bg_ref, proj_w, gate_w,
                                             out_in=False)):
            if casts and k % 3 == 1:
                src, dst = casts.pop()
                dst[...] = src[...].astype(BF16)
            yield
        for src, dst in casts:
            dst[...] = src[...].astype(BF16)

    def step(proj_w, gate_w, proj_r, gate_r):
        _zip_work(scan_tile(proj_r, gate_r), project_and_cast(proj_w, gate_w),
                  n_chunks * SCAN_CHUNK_PIECES, _n_inproj_pieces(w_ref.shape[0]),
                  span_pct=PROJ_SPAN_PCT)

    @pl.when(i == 0)
    def _():
        for _ in project_and_cast(proj_a, gate_a):
            pass

    @pl.when(jnp.logical_and(i > 0, lax.rem(i, 2) == 0))
    def _():
        step(proj_a, gate_a, proj_b, gate_b)

    @pl.when(lax.rem(i, 2) == 1)
    def _():
        step(proj_b, gate_b, proj_a, gate_a)


def _mlstm_long(x, g, w, wg, bg, g_head, batch, seq, tm, casts):
    t, d = x.shape
    n = 4 * d
    dk = d // N_HEADS
    n_tiles = t // tm
    tiles_per_seq = seq // tm
    assert seq % tm == 0 and tm % MLSTM_CHUNK == 0
    prev = lambda i: jnp.maximum(i - 1, 0)
    cast_in, cast_out, cast_shape = zip(*[_cast_specs(a, n_tiles + 1) for a in casts])
    return pl.pallas_call(
        functools.partial(_mlstm_long_kernel, tiles_per_seq=tiles_per_seq, n_cast=len(casts)),
        grid=(n_tiles + 1,),
        in_specs=[pl.BlockSpec((tm, d), lambda i: (jnp.minimum(i, n_tiles - 1), 0)),
                  _resident((1, d)), _resident((n, d)), _resident((GATE_COLS, d)),
                  _resident((1, GATE_COLS)), _resident((1, d)), *cast_in],
        out_specs=[pl.BlockSpec((tm, d), lambda i: (prev(i), 0)),
                   pl.BlockSpec((1, N_HEADS, dk, dk), lambda i: (prev(i) // tiles_per_seq, 0, 0, 0)),
                   pl.BlockSpec((1, SUBLANES, dk), lambda i: (prev(i) // tiles_per_seq, 0, 0)),
                   pl.BlockSpec((1, SUBLANES, LANES), lambda i: (prev(i) // tiles_per_seq, 0, 0)),
                   *cast_out],
        out_shape=[jax.ShapeDtypeStruct((t, d), BF16),
                   jax.ShapeDtypeStruct((batch, N_HEADS, dk, dk), F32),
                   jax.ShapeDtypeStruct((batch, SUBLANES, dk), F32),
                   jax.ShapeDtypeStruct((batch, SUBLANES, LANES), F32), *cast_shape],
        scratch_shapes=[pltpu.VMEM((tm, n), BF16), pltpu.VMEM((tm, GATE_COLS), F32),
                        pltpu.VMEM((tm, n), BF16), pltpu.VMEM((tm, GATE_COLS), F32),
                        pltpu.VMEM((d, n), BF16), pltpu.VMEM((d, GATE_COLS), BF16)],
        compiler_params=_params(1),
        name="mlstm_long",
    )(x, g, w, wg, bg, g_head, *casts)


SCAN_SAMPLE_GROUP = 4


def _n_scan_sample_pieces(bt):
    return 1 + 4 * (bt * N_HEADS // SCAN_SAMPLE_GROUP)


def _scan_sample_pieces(q_ref, k_ref, v_ref, so_ref, gate_ref, gh_ref, c_in, n_in, m_in,
                        hg_ref, c_out, n_out, m_out, *, seq):
    bt = c_in.shape[0]
    dk = q_ref.shape[1] // N_HEADS
    step = pl.program_id(0)
    row = lax.broadcasted_iota(jnp.int32, (seq, seq), 0)
    col = lax.broadcasted_iota(jnp.int32, (seq, seq), 1)
    causal = row >= col
    diag = row == col
    lane = lax.broadcasted_iota(jnp.int32, (1, LANES), 1)
    to_row = lambda x_col: jnp.sum(jnp.where(diag, x_col, 0.0), axis=0, keepdims=True)

    qs, ks, vs = q_ref[...].astype(F32), k_ref[...].astype(F32), v_ref[...].astype(F32)
    sig_o = so_ref[...].astype(F32)
    pairs = [(j, h) for j in range(bt) for h in range(N_HEADS)]
    tok = lambda x, j, h: x[j * seq:(j + 1) * seq, h * dk:(h + 1) * dk]

    gate_q = {}
    for j in range(bt):
        gates = gate_ref[j * seq:(j + 1) * seq, :]
        b_col = _cumsum(_log_sigmoid(gates), 0)
        m_prev_row = m_in[pl.ds(step * bt + j, 1), :]
        m_new_row = jnp.zeros((1, LANES), F32)
        for h in range(N_HEADS):
            li_c, b_c = gates[:, h:h + 1], b_col[:, LANES + h:LANES + h + 1]
            m_prev = m_prev_row[:, h:h + 1]
            b_last = b_c[seq - 1:seq, :]
            dm = jnp.where(causal, b_c - to_row(b_c) + to_row(li_c), -jnp.inf)
            inter = b_c + m_prev
            m_eff = jnp.maximum(inter, jnp.max(dm, axis=1, keepdims=True))
            g = b_last - b_c + li_c
            m_new = jnp.maximum(b_last + m_prev, jnp.max(g, axis=0, keepdims=True))
            gate_q[j, h] = dict(w_inter=jnp.exp(inter - m_eff), p=jnp.exp(dm - m_eff),
                                floor=jnp.exp(-m_eff), decay=jnp.exp(b_last + m_prev - m_new),
                                wg=jnp.exp(g - m_new))
            m_new_row = jnp.where(lane == h, m_new, m_new_row)
        m_out[pl.ds(step * bt + j, 1), :] = m_new_row
    yield

    def grouped(seq_of_pairs):
        for idx, pair in enumerate(seq_of_pairs, start=1):
            yield pair, idx % SCAN_SAMPLE_GROUP == 0

    pack = 2 * SUBLANES
    assert seq <= pack
    as_tile = lambda a: jnp.concatenate(
        [a, jnp.zeros((pack - a.shape[0], a.shape[1]), F32)], axis=0).astype(BF16)
    qx = {}
    for (j, h), pause in grouped(pairs):
        rhs = jnp.concatenate([c_in[j, h].astype(BF16), as_tile(tok(ks, j, h)),
                               as_tile(n_in[j, h:h + 1, :])], axis=0)
        qx[j, h] = lax.dot_general(as_tile(tok(qs, j, h)), rhs, _NT,
                                   preferred_element_type=F32)[:seq]
        if pause:
            yield

    sv, s_sum = {}, {}
    for (j, h), pause in grouped(pairs):
        s = qx[j, h][:, dk:dk + seq] * gate_q[j, h]["p"]
        v = tok(vs, j, h)
        acc = s[:, 0:1] * v[0:1, :]
        for t in range(1, seq):
            acc = acc + s[:, t:t + 1] * v[t:t + 1, :]
        sv[j, h], s_sum[j, h] = acc, jnp.sum(s, axis=1, keepdims=True)
        if pause:
            yield

    for (j, h), pause in grouped(pairs):
        gq = gate_q[j, h]
        num = gq["w_inter"] * qx[j, h][:, :dk] + sv[j, h]
        den = gq["w_inter"] * qx[j, h][:, dk + pack:dk + pack + 1] + s_sum[j, h]
        hh = num * (1.0 / jnp.maximum(jnp.abs(den), gq["floor"]))
        hg_ref[j * seq:(j + 1) * seq, h * dk:(h + 1) * dk] = _gated_head_out(
            hh, gh_ref[:, h * dk:(h + 1) * dk], tok(sig_o, j, h))
        if pause:
            yield

    def update_of(j, h):
        return lax.dot_general(as_tile(tok(vs, j, h) * gate_q[j, h]["wg"]), as_tile(tok(ks, j, h)),
                               _TN, preferred_element_type=F32)

    upd_next = update_of(*pairs[0])
    for idx, ((j, h), pause) in enumerate(grouped(pairs)):
        upd = upd_next
        if idx + 1 < len(pairs):
            upd_next = update_of(*pairs[idx + 1])
        gq = gate_q[j, h]
        c_out[j, h] = gq["decay"] * c_in[j, h] + upd
        n_out[j, h:h + 1, :] = (gq["decay"] * n_in[j, h:h + 1, :]
                                + jnp.sum(tok(ks, j, h) * gq["wg"], axis=0, keepdims=True))
        if pause:
            yield


def _scan_sample_kernel(*refs, seq):
    for _ in _scan_sample_pieces(*refs, seq=seq):
        pass


def _scan_sample_specs(g_head, c0, n0, seq, bt):
    batch = c0.shape[0]
    d = g_head.shape[1]
    dk = d // N_HEADS
    tok = lambda j: pl.BlockSpec((bt * seq, d), lambda i: (i, j))
    c_spec = pl.BlockSpec((bt, N_HEADS, dk, dk), lambda i: (i, 0, 0, 0))
    n_spec = pl.BlockSpec((bt, N_HEADS, dk), lambda i: (i, 0, 0))
    m_spec = pl.BlockSpec((batch, LANES), lambda i: (0, 0))
    in_specs = [tok(0), tok(1), tok(2), tok(3),
                pl.BlockSpec((bt * seq, GATE_COLS), lambda i: (i, 0)),
                _resident((1, d)), c_spec, n_spec, m_spec]
    out_shapes = [jax.ShapeDtypeStruct((batch * seq, d), F32),
                  jax.ShapeDtypeStruct(c0.shape, F32),
                  jax.ShapeDtypeStruct(n0.shape, F32),
                  jax.ShapeDtypeStruct((batch, LANES), F32)]
    return in_specs, [tok(0), c_spec, n_spec, m_spec], out_shapes


def _scan_sample(proj, gates, g_head, c0, n0, m0, seq, bt):
    in_specs, out_specs, out_shapes = _scan_sample_specs(g_head, c0, n0, seq, bt)
    return pl.pallas_call(
        functools.partial(_scan_sample_kernel, seq=seq),
        grid=(c0.shape[0] // bt,),
        in_specs=in_specs, out_specs=out_specs, out_shape=out_shapes,
        compiler_params=_params(1),
        name="mlstm_scan_sample",
    )(proj, proj, proj, proj, gates, g_head, c0, n0, m0)


def _post_pieces(x_ref, z_ref, wo_ref, g_ref, wgu_ref, wd_ref, gfin_ref, out_ref,
                 xn_ref, act_ref, *, final_norm):
    x1 = x_ref[...] + jnp.dot(z_ref[...].astype(BF16), wo_ref[...], preferred_element_type=F32)
    out_ref[...] = x1
    xn_ref[...] = _rmsnorm(x1, g_ref[...]).astype(BF16)
    yield
    ff = wd_ref.shape[0]
    for c in range(ff // FFN_COLS):
        gate = jnp.dot(xn_ref[...], wgu_ref[:, c * FFN_COLS:(c + 1) * FFN_COLS],
                       preferred_element_type=F32)
        up = jnp.dot(xn_ref[...], wgu_ref[:, ff + c * FFN_COLS:ff + (c + 1) * FFN_COLS],
                     preferred_element_type=F32)
        act_ref[:, c * FFN_COLS:(c + 1) * FFN_COLS] = (gate * _sigmoid(gate) * up).astype(BF16)
        yield
    x2 = out_ref[...] + jnp.dot(act_ref[...], wd_ref[...], preferred_element_type=F32)
    if final_norm:
        x2 = _rmsnorm(x2, gfin_ref[...])
    out_ref[...] = x2
    yield


def _n_post_pieces(ff):
    return ff // FFN_COLS + 2


def _post_kernel(*refs, scan_seq):
    if scan_seq is None:
        for _ in _post_pieces(*refs, final_norm=False):
            pass
        return
    post_in, scan_in = refs[:7], refs[7:16]
    out_ref, scan_out = refs[16], refs[17:21]
    xn_ref, act_ref = refs[21:]
    _zip_work(_post_pieces(*post_in, out_ref, xn_ref, act_ref, final_norm=False),
              _scan_sample_pieces(*scan_in, *scan_out, seq=scan_seq),
              _n_post_pieces(post_in[5].shape[0]), _n_scan_sample_pieces(scan_in[6].shape[0]))


def _post(x, z, wo, g, wgu, wd, gfin, layer, tm, scan=None):
    t, d = x.shape
    ff = wd.shape[1]
    assert ff % FFN_COLS == 0
    n_steps = t // tm
    tile = pl.BlockSpec((tm, d), lambda i: (i, 0))
    of_layer = lambda rows, cols: pl.BlockSpec((None, rows, cols), lambda i: (layer, 0, 0),
                                               pipeline_mode=pl.Buffered(1))
    in_specs = [tile, tile, _resident((d, d)), _resident((1, d)),
                of_layer(d, 2 * ff), of_layer(ff, d), _resident((1, d))]
    out_specs, out_shapes, operands = [tile], [jax.ShapeDtypeStruct((t, d), F32)], [x, z, wo, g, wgu, wd, gfin]
    scan_seq = None
    if scan is not None:
        proj, gates, g_head, c0, n0, m0, scan_seq = scan
        assert c0.shape[0] % n_steps == 0
        s_in, s_out, s_shapes = _scan_sample_specs(g_head, c0, n0, scan_seq, c0.shape[0] // n_steps)
        in_specs, out_specs, out_shapes = in_specs + s_in, out_specs + s_out, out_shapes + s_shapes
        operands += [proj, proj, proj, proj, gates, g_head, c0, n0, m0]
    return pl.pallas_call(
        functools.partial(_post_kernel, scan_seq=scan_seq),
        grid=(n_steps,),
        in_specs=in_specs, out_specs=out_specs, out_shape=out_shapes,
        scratch_shapes=[pltpu.VMEM((tm, d), BF16), pltpu.VMEM((tm, ff), BF16)],
        compiler_params=_params(1),
        name="outproj_ffn",
    )(*operands)


def _post_cols_kernel(x_ref, z_ref, wo_ref, g_ref, wg_ref, wu_ref, wd_ref, out_ref, xn_ref):
    @pl.when(pl.program_id(0) == 0)
    def _():
        x1 = x_ref[...] + jnp.dot(z_ref[...].astype(BF16), wo_ref[...], preferred_element_type=F32)
        out_ref[...] = x1
        xn_ref[...] = _rmsnorm(x1, g_ref[...]).astype(BF16)

    gate = jnp.dot(xn_ref[...], wg_ref[...], preferred_element_type=F32)
    up = jnp.dot(xn_ref[...], wu_ref[...], preferred_element_type=F32)
    act = (gate * _sigmoid(gate) * up).astype(BF16)
    out_ref[...] += jnp.dot(act, wd_ref[...], preferred_element_type=F32)


def _post_cols(x, z, wo, g, wgu, wd, layer):
    t, d = x.shape
    ff = wd.shape[1]
    n_chunks = ff // FFN_COLS
    return pl.pallas_call(
        _post_cols_kernel,
        grid=(n_chunks,),
        in_specs=[_resident((t, d)), _resident((t, d)), _resident((d, d)), _resident((1, d)),
                  pl.BlockSpec((None, d, FFN_COLS), lambda c: (layer, 0, c)),
                  pl.BlockSpec((None, d, FFN_COLS), lambda c: (layer, 0, n_chunks + c)),
                  pl.BlockSpec((None, FFN_COLS, d), lambda c: (layer, c, 0))],
        out_specs=pl.BlockSpec((t, d), lambda c: (0, 0)),
        out_shape=jax.ShapeDtypeStruct((t, d), F32),
        scratch_shapes=[pltpu.VMEM((t, d), BF16)],
        compiler_params=_params(1),
        name="outproj_ffn_cols",
    )(x, z, wo, g, wgu, wgu, wd)


def _conv_proj(x_ref, g_ref, w_ref):
    d = x_ref.shape[1]
    xn = _rmsnorm(x_ref[...], g_ref[...]).astype(BF16)
    part = lambda c: jnp.dot(xn, w_ref[:, c * d:(c + 1) * d], preferred_element_type=F32)
    return part(0), part(1) * part(2)


def _conv_long_pieces(x_ref, g_ref, w_ref, taps_ref, st_out_ref, carry_ref, z_ref, x_copy_ref):
    tm, d = x_ref.shape
    x = x_ref[...]
    x_copy_ref[...] = x
    xn = _rmsnorm(x, g_ref[...]).astype(BF16)
    row = lax.broadcasted_iota(jnp.int32, (tm, 1), 0)
    yield
    for c0 in range(0, d, INPROJ_COLS):
        cols = slice(c0, c0 + INPROJ_COLS)
        part = lambda k: jnp.dot(xn, w_ref[:, k * d + c0:k * d + c0 + INPROJ_COLS],
                                 preferred_element_type=F32)
        bg, u = part(0), part(1) * part(2)
        yield
        s0, s1 = carry_ref[0:1, cols], carry_ref[1:2, cols]
        prev1 = jnp.where(row == 0, s1, pltpu.roll(u, 1, 0))
        prev2 = jnp.where(row == 0, s0, jnp.where(row == 1, s1, pltpu.roll(u, 2, 0)))
        conv = taps_ref[0:1, cols] * prev2 + taps_ref[1:2, cols] * prev1 + taps_ref[2:3, cols] * u
        z_ref[:, cols] = (bg * conv).astype(BF16)
        tail = u[tm - (CONV_W - 1):tm, :]
        carry_ref[:, cols] = tail
        st_out_ref[0, :, cols] = tail
        yield


def _n_conv_long_pieces(d):
    return 1 + 2 * (d // INPROJ_COLS)


def _layer1_long_kernel(x_ref, gmix_ref, wcin_ref, taps_ref, st_ref, wo_ref, gffn_ref, wgu_ref,
                        wd_ref, gfin_ref, out_ref, st_out_ref,
                        carry_ref, z_a, x_a, z_b, x_b, xn_ref, act_ref, *, tiles_per_seq, n_tiles):
    i = pl.program_id(0)

    @pl.when(jnp.logical_and(i < n_tiles, lax.rem(i, tiles_per_seq) == 0))
    def _():
        carry_ref[...] = st_ref[0]

    def conv_tile(z_w, x_w):
        return _conv_long_pieces(x_ref, gmix_ref, wcin_ref, taps_ref, st_out_ref, carry_ref, z_w, x_w)

    def post_tile(z_r, x_r):
        return _post_pieces(x_r, z_r, wo_ref, gffn_ref, wgu_ref, wd_ref, gfin_ref, out_ref,
                            xn_ref, act_ref, final_norm=True)

    def step(z_w, x_w, z_r, x_r):
        _zip_work(post_tile(z_r, x_r), conv_tile(z_w, x_w),
                  _n_post_pieces(wd_ref.shape[0]), _n_conv_long_pieces(x_ref.shape[1]))

    @pl.when(i == 0)
    def _():
        for _ in conv_tile(z_a, x_a):
            pass

    @pl.when(jnp.logical_and(i > 0, lax.rem(i, 2) == 0))
    def _():
        step(z_a, x_a, z_b, x_b)

    @pl.when(lax.rem(i, 2) == 1)
    def _():
        step(z_b, x_b, z_a, x_a)


def _layer1_long(x, gmix, wcin, taps, state, wo, gffn, wgu, wd, gfin, layer, batch, seq, tm):
    t, d = x.shape
    ff = wd.shape[1]
    keep = CONV_W - 1
    n_tiles = t // tm
    tiles_per_seq = seq // tm
    assert seq % tm == 0
    cur = lambda i: jnp.minimum(i, n_tiles - 1)
    st_spec = pl.BlockSpec((1, keep, d), lambda i: (cur(i) // tiles_per_seq, 0, 0))
    of_layer = lambda rows, cols: pl.BlockSpec((None, rows, cols), lambda i: (layer, 0, 0),
                                               pipeline_mode=pl.Buffered(1))
    return pl.pallas_call(
        functools.partial(_layer1_long_kernel, tiles_per_seq=tiles_per_seq, n_tiles=n_tiles),
        grid=(n_tiles + 1,),
        in_specs=[pl.BlockSpec((tm, d), lambda i: (cur(i), 0)),
                  _resident((1, d)), _resident((d, 3 * d)), _resident((SUBLANES, d)), st_spec,
                  _resident((d, d)), _resident((1, d)), of_layer(d, 2 * ff), of_layer(ff, d),
                  _resident((1, d))],
        out_specs=[pl.BlockSpec((tm, d), lambda i: (jnp.maximum(i - 1, 0), 0)), st_spec],
        out_shape=[jax.ShapeDtypeStruct((t, d), F32), jax.ShapeDtypeStruct((batch, keep, d), F32)],
        scratch_shapes=[pltpu.VMEM((keep, d), F32),
                        pltpu.VMEM((tm, d), BF16), pltpu.VMEM((tm, d), F32),
                        pltpu.VMEM((tm, d), BF16), pltpu.VMEM((tm, d), F32),
                        pltpu.VMEM((tm, d), BF16), pltpu.VMEM((tm, ff), BF16)],
        compiler_params=_params(1),
        name="layer1_long",
    )(x, gmix, wcin, taps, state, wo, gffn, wgu, wd, gfin)


def _conv_short_kernel(x_ref, g_ref, w_ref, taps_ref, st_ref, z_ref, st_out_ref, *, seq):
    tm, d = x_ref.shape
    nseq = tm // seq
    bg, u = _conv_proj(x_ref, g_ref, w_ref)
    pos = lax.broadcasted_iota(jnp.int32, (1, seq, 1), 1)
    s0, s1 = st_ref[:, 0:1, :], st_ref[:, 1:2, :]
    u3 = u.reshape(nseq, seq, d)
    r1 = pltpu.roll(u, 1, 0).reshape(nseq, seq, d)
    r2 = pltpu.roll(u, 2, 0).reshape(nseq, seq, d)
    prev1 = jnp.where(pos == 0, s1, r1)
    prev2 = jnp.where(pos == 0, s0, jnp.where(pos == 1, s1, r2))
    conv = (taps_ref[0:1, :].reshape(1, 1, d) * prev2 + taps_ref[1:2, :].reshape(1, 1, d) * prev1
            + taps_ref[2:3, :].reshape(1, 1, d) * u3)
    z_ref[...] = (bg * conv.reshape(tm, d)).astype(BF16)
    st_out_ref[...] = u3[:, seq - (CONV_W - 1):seq, :]


def _layer1_short_kernel(x_ref, gmix_ref, wcin_ref, taps_ref, st_ref, wo_ref, gffn_ref, wgu_ref,
                         wd_ref, gfin_ref, out_ref, st_out_ref, z_ref, xn_ref, act_ref, *, seq):
    _conv_short_kernel(x_ref, gmix_ref, wcin_ref, taps_ref, st_ref, z_ref, st_out_ref, seq=seq)
    for _ in _post_pieces(x_ref, z_ref, wo_ref, gffn_ref, wgu_ref, wd_ref, gfin_ref, out_ref,
                          xn_ref, act_ref, final_norm=True):
        pass


def _layer1_short(x, gmix, wcin, taps, state, wo, gffn, wgu, wd, gfin, layer, batch, seq, tm):
    t, d = x.shape
    ff = wd.shape[1]
    keep = CONV_W - 1
    assert tm % seq == 0
    st_spec = pl.BlockSpec((tm // seq, keep, d), lambda i: (i, 0, 0))
    tile = pl.BlockSpec((tm, d), lambda i: (i, 0))
    of_layer = lambda rows, cols: pl.BlockSpec((None, rows, cols), lambda i: (layer, 0, 0),
                                               pipeline_mode=pl.Buffered(1))
    return pl.pallas_call(
        functools.partial(_layer1_short_kernel, seq=seq),
        grid=(t // tm,),
        in_specs=[tile, _resident((1, d)), _resident((d, 3 * d)), _resident((SUBLANES, d)), st_spec,
                  _resident((d, d)), _resident((1, d)), of_layer(d, 2 * ff), of_layer(ff, d),
                  _resident((1, d))],
        out_specs=[tile, st_spec],
        out_shape=[jax.ShapeDtypeStruct((t, d), F32), jax.ShapeDtypeStruct((batch, keep, d), F32)],
        scratch_shapes=[pltpu.VMEM((tm, d), BF16), pltpu.VMEM((tm, d), BF16),
                        pltpu.VMEM((tm, ff), BF16)],
        compiler_params=_params(1),
        name="layer1_short",
    )(x, gmix, wcin, taps, state, wo, gffn, wgu, wd, gfin)


def _token_tile(t):
    return min(t, 512)


def _layer1(x1, conv0, wts, batch, seq, tm):
    layer1 = _layer1_long if seq >= tm else _layer1_short
    y, conv_fin = layer1(x1, wts["norm_mix1"], wts["w_conv_in"], wts["taps"], conv0,
                         wts["w_conv_out"], wts["norm_ffn1"], wts["w_gu"], wts["w_down"],
                         wts["norm_final"], 1, batch, seq, tm)
    return y.reshape(batch, seq, -1), conv_fin[None]


def kernel(x_prompt, x_sample, state_mlstm_C, state_mlstm_n, state_mlstm_m, state_conv, norm_mix, norm_ffn, norm_final, w_mlstm_in, b_mlstm_gate, mlstm_head_norm, w_mlstm_out, w_conv_in, w_conv_taps, w_conv_out, w_ffn_gate_up, w_ffn_down):
    d = x_prompt.shape[-1]
    dk = d // N_HEADS
    assert norm_mix.shape[0] == 2 and w_mlstm_in.shape[0] == 1 and w_conv_in.shape[0] == 1
    assert w_conv_taps.shape[1] == CONV_W and 2 * N_HEADS <= LANES
    row = lambda v: v.reshape(1, -1).astype(F32)
    pad_cols = lambda m: jnp.pad(m, ((0, 0), (0, LANES - N_HEADS)))
    pad_rows = lambda m: jnp.pad(m, ((0, LANES - N_HEADS), (0, 0)))
    w_in_t = jnp.swapaxes(w_mlstm_in[0], 0, 1)
    q_scale = jnp.concatenate([jnp.full((d, 1), dk ** -0.5, F32),
                               jnp.ones((w_in_t.shape[0] - d, 1), F32)])
    w_in_bf = (w_in_t * q_scale).astype(BF16)
    wts = {
        "norm_mix0": row(norm_mix[0]), "norm_mix1": row(norm_mix[1]),
        "norm_ffn0": row(norm_ffn[0]), "norm_ffn1": row(norm_ffn[1]),
        "norm_final": row(norm_final), "g_head": row(mlstm_head_norm[0]),
        "w_in": w_in_bf,
        "w_gate": jnp.concatenate([pad_rows(w_in_bf[4 * d:4 * d + N_HEADS]),
                                   pad_rows(w_in_bf[4 * d + N_HEADS:])], axis=0),
        "b_gate": jnp.concatenate([pad_cols(b_mlstm_gate[:, :N_HEADS]),
                                   pad_cols(b_mlstm_gate[:, N_HEADS:])], axis=1).astype(F32),
        "taps": jnp.pad(w_conv_taps[0], ((0, SUBLANES - CONV_W), (0, 0))).astype(F32),
    }
    bp, sp, _ = x_prompt.shape
    bs, ss, _ = x_sample.shape
    tp, ts = bp * sp, bs * ss
    tm_p, tm_s = _token_tile(tp), _token_tile(ts)
    xp, xs = x_prompt.reshape(tp, d), x_sample.reshape(ts, d)
    n_layers, _, two_ff = w_ffn_gate_up.shape
    ff = two_ff // 2

    hg_p, c_p, n_pad, m_pad, w_gu, w_down, wts["w_conv_in"], wts["w_conv_out"], w_out0 = _mlstm_long(
        xp, wts["norm_mix0"], wts["w_in"], wts["w_gate"], wts["b_gate"], wts["g_head"], bp, sp, tm_p,
        casts=[w_ffn_gate_up.reshape(n_layers * d, two_ff), w_ffn_down.reshape(n_layers * ff, d),
               w_conv_in[0], w_conv_out[0], w_mlstm_out[0]])
    wts["w_gu"], wts["w_down"] = w_gu.reshape(n_layers, d, two_ff), w_down.reshape(n_layers, ff, d)
    n_p, m_p = n_pad[:, :N_HEADS, :], m_pad[:, 0, :N_HEADS]

    proj_s, gates_s = _inproj(xs, wts["norm_mix0"], wts["w_in"], wts["w_gate"], wts["b_gate"], tm_s)
    m0_pad = jnp.pad(state_mlstm_m[0], ((0, 0), (0, LANES - N_HEADS)))
    scan_args = (proj_s, gates_s, wts["g_head"], state_mlstm_C[0], state_mlstm_n[0], m0_pad)
    post0 = functools.partial(_post, wo=w_out0, g=wts["norm_ffn0"], wgu=wts["w_gu"], wd=wts["w_down"],
                              gfin=wts["norm_final"], layer=0)
    if bs % (tp // tm_p) == 0:
        x1_p, hg_s, c_s, n_s, m_s_pad = post0(xp, hg_p, tm=tm_p, scan=scan_args + (ss,))
    else:
        x1_p, = post0(xp, hg_p, tm=tm_p)
        hg_s, c_s, n_s, m_s_pad = _scan_sample(*scan_args, ss, 4)
    x1_s = _post_cols(xs, hg_s, w_out0, wts["norm_ffn0"], wts["w_gu"], wts["w_down"], 0)

    y_p, conv_p = _layer1(x1_p, jnp.zeros((bp, CONV_W - 1, d), F32), wts, bp, sp, tm_p)
    y_s, conv_s = _layer1(x1_s, state_conv[0], wts, bs, ss, tm_s)
    return (y_p, y_s, c_p[None], n_p[None], m_p[None], conv_p,
            c_s[None], n_s[None], m_s_pad[:, :N_HEADS][None], conv_s)
```
